```python
import math
import jax, jax.numpy as jnp
from jax import lax
import numpy as np

D_MODEL = 1024
BATCH = 4
SEQ = 8192
DEPTH = 2

MLSTM_HEADS = 4
MLSTM_HEAD_DIM = 256
MLSTM_WIDTH = MLSTM_HEADS * MLSTM_HEAD_DIM
MLSTM_CONV = 4
MLSTM_CHUNK = 64
MOBA_HEADS = 8
MOBA_HEAD_DIM = 128
MOBA_WIDTH = MOBA_HEADS * MOBA_HEAD_DIM
MOBA_BLOCK = 256
MOBA_TOPK = 3
MOBA_Q_CHUNK = 32
NUM_BUCKETS = 32
REL_MAX_DISTANCE = 128
D_FF = 2816
FFN_CONV = 3
ALPHA = (2 * DEPTH) ** 0.25
BETA = (8 * DEPTH) ** -0.25
LN_EPS = 1e-5
SPLIT_SIZES = (MLSTM_WIDTH, MLSTM_WIDTH, MLSTM_WIDTH, MLSTM_WIDTH,
               MLSTM_HEADS, MLSTM_HEADS,
               MOBA_WIDTH, MOBA_WIDTH, MOBA_WIDTH,
               D_MODEL, D_MODEL)
N_IN = 4 * MLSTM_WIDTH + 2 * MLSTM_HEADS + 3 * MOBA_WIDTH + 2 * D_MODEL
MLSTM_V_START = 2 * MLSTM_WIDTH
MLSTM_F_START = 4 * MLSTM_WIDTH + MLSTM_HEADS
MOBA_V_START = 4 * MLSTM_WIDTH + 2 * MLSTM_HEADS + 2 * MOBA_WIDTH

kernel_name = "hybrid_mlstm_moba_convffn_deepnorm"


def _split_cols(t, sizes):
    idx, acc = [], 0
    for s in sizes[:-1]:
        acc += s
        idx.append(acc)
    return jnp.split(t, idx, axis=-1)


def layer_norm(x, g, b):
    xf = x.astype(jnp.float32)
    mu = xf.mean(-1, keepdims=True)
    var = jnp.square(xf - mu).mean(-1, keepdims=True)
    return ((xf - mu) * lax.rsqrt(var + LN_EPS) * g + b).astype(x.dtype)


def causal_dwconv(x, w):
    k, c = w.shape
    return lax.conv_general_dilated(
        x, w[:, None, :], window_strides=(1,), padding=((k - 1, 0),),
        dimension_numbers=("NWC", "WIO", "NWC"), feature_group_count=c)


def to_heads(t, h, d):
    b, s, _ = t.shape
    return t.reshape(b, s, h, d).transpose(0, 2, 1, 3)


def mlstm_chunkwise(q, k, v, i_pre, f_pre):
    B, H, S, Dh = q.shape
    L = MLSTM_CHUNK
    nc = S // L

    def chunks(t):
        return jnp.moveaxis(t.reshape(B, H, nc, L, *t.shape[3:]), 2, 0)

    qf = q.astype(jnp.float32)
    kf = k.astype(jnp.float32) * (Dh ** -0.5)
    vf = v.astype(jnp.float32)
    log_f = jax.nn.log_sigmoid(f_pre.astype(jnp.float32))
    log_i = i_pre.astype(jnp.float32)
    b_cum = jnp.cumsum(chunks(log_f), axis=-1)
    causal = jnp.tril(jnp.ones((L, L), dtype=bool))

    def step(carry, xs):
        C, n, m = carry
        q_c, k_c, v_c, b_c, li_c = xs
        d_intra = jnp.where(causal, b_c[..., :, None] - b_c[..., None, :] + li_c[..., None, :], -jnp.inf)
        inter = b_c + m[..., None]
        m_q = jnp.maximum(inter, d_intra.max(-1))
        w_inter = jnp.exp(inter - m_q)
        s = jnp.einsum("bhld,bhsd->bhls", q_c, k_c) * jnp.exp(d_intra - m_q[..., None])
        num = w_inter[..., None] * jnp.einsum("bhld,bhde->bhle", q_c, C) + jnp.einsum("bhls,bhse->bhle", s, v_c)
        den = w_inter * jnp.einsum("bhld,bhd->bhl", q_c, n) + s.sum(-1)
        h = num / jnp.maximum(jnp.abs(den), jnp.exp(-m_q))[..., None]
        b_last = b_c[..., -1]
        d_state = b_last[..., None] - b_c + li_c
        m_new = jnp.maximum(b_last + m, d_state.max(-1))
        w_prev = jnp.exp(b_last + m - m_new)
        w_k = jnp.exp(d_state - m_new[..., None])
        C = w_prev[..., None, None] * C + jnp.einsum("bhl,bhld,bhle->bhde", w_k, k_c, v_c)
        n = w_prev[..., None] * n + jnp.einsum("bhl,bhld->bhd", w_k, k_c)
        return (C, n, m_new), h

    init = (jnp.zeros((B, H, Dh, Dh), jnp.float32), jnp.zeros((B, H, Dh), jnp.float32),
            jnp.zeros((B, H), jnp.float32))
    _, h = lax.scan(step, init, (chunks(qf), chunks(kf), chunks(vf), b_cum, chunks(log_i)))
    return jnp.moveaxis(h, 0, 2).reshape(B, H, S, Dh)


def head_norm(h, gain):
    mu = h.mean(-1, keepdims=True)
    var = jnp.square(h - mu).mean(-1, keepdims=True)
    hn = (h - mu) * lax.rsqrt(var + LN_EPS)
    B, H, S, Dh = h.shape
    return hn.transpose(0, 2, 1, 3).reshape(B, S, H * Dh) * gain.astype(jnp.float32)


def t5_bucket(rel):
    n = jnp.maximum(rel, 0)
    max_exact = NUM_BUCKETS // 2
    nf = jnp.maximum(n, max_exact).astype(jnp.float32)
    large = max_exact + (jnp.log(nf / max_exact) / math.log(REL_MAX_DISTANCE / max_exact)
                         * (NUM_BUCKETS - max_exact)).astype(jnp.int32)
    large = jnp.minimum(large, NUM_BUCKETS - 1)
    return jnp.where(n < max_exact, n, large)


def moba_attention(q, k, v, rel_bias):
    B, H, S, Dh = q.shape
    nb = -(-S // MOBA_BLOCK)
    s_pad = nb * MOBA_BLOCK
    pad = ((0, 0), (0, 0), (0, s_pad - S), (0, 0))
    k_blocks = jnp.pad(k, pad).reshape(B, H, nb, MOBA_BLOCK, Dh)
    v_blocks = jnp.pad(v, pad).reshape(B, H, nb, MOBA_BLOCK, Dh)
    k_mean = k_blocks.astype(jnp.float32).mean(axis=3)
    topk = min(MOBA_TOPK, nb)
    scale = Dh ** -0.5
    qc = MOBA_Q_CHUNK
    blk_pos = jnp.arange(MOBA_BLOCK, dtype=jnp.int32)
    b_idx = jnp.arange(B)[:, None, None, None]
    h_idx = jnp.arange(H)[None, :, None, None]
    h_idx5 = jnp.arange(H)[None, :, None, None, None]

    def chunk(c):
        start = c * qc
        q_c = lax.dynamic_slice_in_dim(q, start, qc, axis=2)
        q_pos = start + jnp.arange(qc, dtype=jnp.int32)
        own = start // MOBA_BLOCK
        gate = jnp.einsum("bhqd,bhnd->bhqn", q_c.astype(jnp.float32), k_mean)
        gate = jnp.where(jnp.arange(nb) < own, gate, -jnp.inf)
        _, sel = lax.top_k(gate, topk)
        valid = jnp.arange(topk) < own
        k_sel = k_blocks[b_idx, h_idx, sel]
        v_sel = v_blocks[b_idx, h_idx, sel]
        k_pos = sel[..., None] * MOBA_BLOCK + blk_pos
        bias_sel = rel_bias[t5_bucket(q_pos[:, None, None] - k_pos), h_idx5].astype(jnp.float32)
        logit_sel = jnp.einsum("bhqd,bhqtkd->bhqtk", q_c, k_sel).astype(jnp.float32) * scale + bias_sel
        logit_sel = jnp.where(valid[:, None], logit_sel, -jnp.inf)
        k_own = lax.dynamic_index_in_dim(k_blocks, own, axis=2, keepdims=False)
        v_own = lax.dynamic_index_in_dim(v_blocks, own, axis=2, keepdims=False)
        rel_own = q_pos[:, None] - (own * MOBA_BLOCK + blk_pos)[None, :]
        bias_own = jnp.moveaxis(rel_bias[t5_bucket(rel_own)], -1, 0).astype(jnp.float32)
        logit_own = jnp.einsum("bhqd,bhkd->bhqk", q_c, k_own).astype(jnp.float32) * scale + bias_own
        logit_own = jnp.where(rel_own >= 0, logit_own, -jnp.inf)
        logits = jnp.concatenate([logit_sel.reshape(B, H, qc, topk * MOBA_BLOCK), logit_own], axis=-1)
        p = jax.nn.softmax(logits, axis=-1)
        p_sel = p[..., :topk * MOBA_BLOCK].reshape(B, H, qc, topk, MOBA_BLOCK).astype(v.dtype)
        p_own = p[..., topk * MOBA_BLOCK:].astype(v.dtype)
        return (jnp.einsum("bhqtk,bhqtkd->bhqd", p_sel, v_sel)
                + jnp.einsum("bhqk,bhkd->bhqd", p_own, v_own))

    outs = lax.map(chunk, jnp.arange(S // qc))
    return jnp.moveaxis(outs, 0, 2).reshape(B, H, S, Dh)


def hybrid_mixer(x, w_in, b_in, conv_qk, mlstm_norm, rel_bias, w_branch_a, w_branch_b, w_out):
    B, S, _ = x.shape
    proj = jnp.einsum("bsd,dn->bsn", x, w_in) + b_in
    q_m, k_m, v_m, o_m, i_m, f_m, q_b, k_b, v_b, g_a, g_b = _split_cols(proj, SPLIT_SIZES)
    qk = jax.nn.silu(causal_dwconv(jnp.concatenate([q_m, k_m], axis=-1), conv_qk))
    q_m, k_m = jnp.split(qk, 2, axis=-1)
    h_m = mlstm_chunkwise(to_heads(q_m, MLSTM_HEADS, MLSTM_HEAD_DIM),
                         to_heads(k_m, MLSTM_HEADS, MLSTM_HEAD_DIM),
                         to_heads(v_m, MLSTM_HEADS, MLSTM_HEAD_DIM),
                         i_m.transpose(0, 2, 1), f_m.transpose(0, 2, 1))
    y_a = (jax.nn.sigmoid(o_m.astype(jnp.float32)) * head_norm(h_m, mlstm_norm)).astype(x.dtype)
    h_b = moba_attention(to_heads(q_b, MOBA_HEADS, MOBA_HEAD_DIM),
                         to_heads(k_b, MOBA_HEADS, MOBA_HEAD_DIM),
                         to_heads(v_b, MOBA_HEADS, MOBA_HEAD_DIM), rel_bias)
    y_b = h_b.transpose(0, 2, 1, 3).reshape(B, S, MOBA_WIDTH)
    merged = (jax.nn.sigmoid(g_a) * jnp.einsum("bsw,wd->bsd", y_a, w_branch_a)
              + jax.nn.sigmoid(g_b) * jnp.einsum("bsw,wd->bsd", y_b, w_branch_b))
    return jnp.einsum("bsd,de->bse", merged, w_out)


def conv_gated_mlp(x, w_up, conv_ffn, w_down):
    u = causal_dwconv(jnp.einsum("bsd,df->bsf", x, w_up), conv_ffn)
    g, up = jnp.split(u, 2, axis=-1)
    return jnp.einsum("bsf,fd->bsd", jax.nn.silu(g) * up, w_down)


def setup_inputs(seed: int = 0) -> dict:
    key = jax.random.key(seed)
    ks = jax.random.split(key, 16)
    nrm = jax.random.normal
    x = nrm(ks[0], (BATCH, SEQ, D_MODEL), jnp.float32)
    col_scale = jnp.ones((N_IN,), jnp.float32)
    col_scale = col_scale.at[MLSTM_V_START:MLSTM_V_START + MLSTM_WIDTH].set(BETA)
    col_scale = col_scale.at[MOBA_V_START:MOBA_V_START + MOBA_WIDTH].set(BETA)
    w_in = nrm(ks[1], (DEPTH, D_MODEL, N_IN), jnp.float32) * (D_MODEL ** -0.5) * col_scale
    b_in = 0.02 * nrm(ks[2], (DEPTH, N_IN), jnp.float32)
    b_in = b_in.at[:, MLSTM_F_START:MLSTM_F_START + MLSTM_HEADS].add(
        jnp.linspace(3.0, 6.0, MLSTM_HEADS, dtype=jnp.float32))
    conv_qk = nrm(ks[3], (DEPTH, MLSTM_CONV, 2 * MLSTM_WIDTH), jnp.float32) * (MLSTM_CONV ** -0.5)
    mlstm_norm = 1.0 + 0.02 * nrm(ks[4], (DEPTH, MLSTM_WIDTH), jnp.float32)
    rel_bias = 0.1 * nrm(ks[5], (NUM_BUCKETS, MOBA_HEADS), jnp.float32)
    w_branch_a = nrm(ks[6], (DEPTH, MLSTM_WIDTH, D_MODEL), jnp.float32) * (MLSTM_WIDTH ** -0.5)
    w_branch_b = nrm(ks[7], (DEPTH, MOBA_WIDTH, D_MODEL), jnp.float32) * (MOBA_WIDTH ** -0.5)
    w_out = nrm(ks[8], (DEPTH, D_MODEL, D_MODEL), jnp.float32) * (D_MODEL ** -0.5) * BETA
    ln1_g = 1.0 + 0.02 * nrm(ks[9], (DEPTH, D_MODEL), jnp.float32)
    ln1_b = 0.02 * nrm(ks[10], (DEPTH, D_MODEL), jnp.float32)
    w_up = nrm(ks[11], (DEPTH, D_MODEL, 2 * D_FF), jnp.float32) * (D_MODEL ** -0.5)
    conv_ffn = nrm(ks[12], (DEPTH, FFN_CONV, 2 * D_FF), jnp.float32) * (FFN_CONV ** -0.5)
    w_down = nrm(ks[13], (DEPTH, D_FF, D_MODEL), jnp.float32) * (D_FF ** -0.5) * BETA
    ln2_g = 1.0 + 0.02 * nrm(ks[14], (DEPTH, D_MODEL), jnp.float32)
    ln2_b = 0.02 * nrm(ks[15], (DEPTH, D_MODEL), jnp.float32)
    return {"x": x, "w_in": w_in, "b_in": b_in, "conv_qk": conv_qk, "mlstm_norm": mlstm_norm,
            "rel_bias": rel_bias, "w_branch_a": w_branch_a, "w_branch_b": w_branch_b,
            "w_out": w_out, "ln1_g": ln1_g, "ln1_b": ln1_b, "w_up": w_up, "conv_ffn": conv_ffn,
            "w_down": w_down, "ln2_g": ln2_g, "ln2_b": ln2_b}


def reference(x, w_in, b_in, conv_qk, mlstm_norm, rel_bias, w_branch_a, w_branch_b, w_out,
              ln1_g, ln1_b, w_up, conv_ffn, w_down, ln2_g, ln2_b):
    for l in range(DEPTH):
        mix = hybrid_mixer(x, w_in[l], b_in[l], conv_qk[l], mlstm_norm[l], rel_bias,
                           w_branch_a[l], w_branch_b[l], w_out[l])
        x = layer_norm(ALPHA * x + mix, ln1_g[l], ln1_b[l])
        ffn = conv_gated_mlp(x, w_up[l], conv_ffn[l], w_down[l])
        x = layer_norm(ALPHA * x + ffn, ln2_g[l], ln2_b[l])
    return x
```

```python
import functools
import math

import jax
import jax.numpy as jnp
from jax import lax
from jax.experimental import pallas as pl
from jax.experimental.pallas import tpu as pltpu

D_MODEL = 1024
DEPTH = 2
MLSTM_HEADS = 4
MLSTM_HEAD_DIM = 256
MLSTM_WIDTH = MLSTM_HEADS * MLSTM_HEAD_DIM
MLSTM_CONV = 4
MOBA_HEADS = 8
MOBA_HEAD_DIM = 128
MOBA_WIDTH = MOBA_HEADS * MOBA_HEAD_DIM
MOBA_BLOCK = 256
MOBA_TOPK = 3
NUM_BUCKETS = 32
REL_MAX_DISTANCE = 128
D_FF = 2816
FFN_CONV = 3
ALPHA = (2 * DEPTH) ** 0.25
LN_EPS = 1e-5

SUBLANES = 8
LANES = 128
VMEM_LIMIT_BYTES = 56 * 1024 * 1024

MLSTM_CHUNK = 256
FFN_CHUNK = 256
N_FFN_CHUNKS = D_FF // FFN_CHUNK
PROJ_CHUNK = 512
NEG = -1e30

BF16 = jnp.bfloat16
F32 = jnp.float32


def _dot(a, b):
    return jnp.dot(a, b, preferred_element_type=F32)


def _dot_nt(a, b):
    return lax.dot_general(a, b, (((1,), (1,)), ((), ())), preferred_element_type=F32)


def _resident(shape):
    nd = len(shape)
    return pl.BlockSpec(shape, lambda *_: (0,) * nd, pipeline_mode=pl.Buffered(1))


def _layer_norm_rows(z, g, b):
    mu = jnp.mean(z, axis=-1, keepdims=True)
    zc = z - mu
    var = jnp.mean(zc * zc, axis=-1, keepdims=True)
    return zc * lax.rsqrt(var + LN_EPS) * g + b


def _inproj_kernel(x_ref, wqk_ref, bqk_ref, cw_ref, wvo_ref, bvo_ref, wif_ref, bif_ref,
                   wb_ref, bb_ref, wg_ref, bg_ref,
                   qk_ref, vo_ref, if_ref, qkvb_ref, gg_ref, ubuf_ref, *, tm, tiles_per_seq):
    i = pl.program_id(0)
    seq_start = (i % tiles_per_seq) == 0
    xb = x_ref[...].astype(BF16)
    halo = MLSTM_CONV - 1

    for c in range(2 * MLSTM_WIDTH // PROJ_CHUNK):
        cs = slice(c * PROJ_CHUNK, (c + 1) * PROJ_CHUNK)
        u = _dot(xb, wqk_ref[:, cs]) + bqk_ref[:, cs]

        @pl.when(seq_start)
        def _():
            ubuf_ref[c, 0:SUBLANES, :] = jnp.zeros((SUBLANES, PROJ_CHUNK), F32)

        @pl.when(jnp.logical_not(seq_start))
        def _():
            ubuf_ref[c, 0:SUBLANES, :] = ubuf_ref[c, tm:tm + SUBLANES, :]

        ubuf_ref[c, SUBLANES:SUBLANES + tm, :] = u
        conv = cw_ref[halo:halo + 1, cs] * u
        for k in range(halo):
            off = SUBLANES - halo + k
            conv = conv + cw_ref[k:k + 1, cs] * ubuf_ref[c, off:off + tm, :]
        act = conv * jax.nn.sigmoid(conv)
        if c * PROJ_CHUNK >= MLSTM_WIDTH:
            act = act * (MLSTM_HEAD_DIM ** -0.5)
        qk_ref[:, cs] = act.astype(qk_ref.dtype)

    for c in range(2 * MLSTM_WIDTH // PROJ_CHUNK):
        cs = slice(c * PROJ_CHUNK, (c + 1) * PROJ_CHUNK)
        vo_ref[:, cs] = (_dot(xb, wvo_ref[:, cs]) + bvo_ref[:, cs]).astype(vo_ref.dtype)
    if_ref[...] = _dot(xb, wif_ref[...]) + bif_ref[...]
    for c in range(3 * MOBA_WIDTH // PROJ_CHUNK):
        cs = slice(c * PROJ_CHUNK, (c + 1) * PROJ_CHUNK)
        qkvb_ref[:, cs] = (_dot(xb, wb_ref[:, cs]) + bb_ref[:, cs]).astype(qkvb_ref.dtype)
    for c in range(2 * D_MODEL // PROJ_CHUNK):
        cs = slice(c * PROJ_CHUNK, (c + 1) * PROJ_CHUNK)
        gg_ref[:, cs] = (_dot(xb, wg_ref[:, cs]) + bg_ref[:, cs]).astype(gg_ref.dtype)


def _in_proj(x2d, w_in, b_in, conv_qk, seq_len, tm=256):
    m_rows = x2d.shape[0]
    w4 = 4 * MLSTM_WIDTH
    nh = MLSTM_HEADS
    b0 = w4 + 2 * nh
    scale_b = MOBA_HEAD_DIM ** -0.5
    col_scale = jnp.concatenate([jnp.full((MOBA_WIDTH,), scale_b, F32), jnp.ones((2 * MOBA_WIDTH,), F32)])
    wqk = w_in[:, :2 * MLSTM_WIDTH].astype(BF16)
    bqk = b_in[None, :2 * MLSTM_WIDTH]
    wvo = w_in[:, 2 * MLSTM_WIDTH:w4].astype(BF16)
    bvo = b_in[None, 2 * MLSTM_WIDTH:w4]
    wif = jnp.pad(w_in[:, w4:b0], ((0, 0), (0, LANES - 2 * nh))).astype(BF16)
    bif = jnp.pad(b_in[w4:b0], (0, LANES - 2 * nh))[None, :]
    wb = (w_in[:, b0:b0 + 3 * MOBA_WIDTH] * col_scale).astype(BF16)
    bb = (b_in[b0:b0 + 3 * MOBA_WIDTH] * col_scale)[None, :]
    wg = w_in[:, b0 + 3 * MOBA_WIDTH:].astype(BF16)
    bg = b_in[None, b0 + 3 * MOBA_WIDTH:]

    row = lambda n: pl.BlockSpec((tm, n), lambda i: (i, 0))
    n_chunks = 2 * MLSTM_WIDTH // PROJ_CHUNK
    return pl.pallas_call(
        functools.partial(_inproj_kernel, tm=tm, tiles_per_seq=seq_len // tm),
        grid=(m_rows // tm,),
        in_specs=[row(D_MODEL),
                  _resident(wqk.shape), _resident(bqk.shape), _resident(conv_qk.shape),
                  _resident(wvo.shape), _resident(bvo.shape),
                  _resident(wif.shape), _resident(bif.shape),
                  _resident(wb.shape), _resident(bb.shape),
                  _resident(wg.shape), _resident(bg.shape)],
        out_specs=[row(2 * MLSTM_WIDTH), row(2 * MLSTM_WIDTH), row(LANES), row(3 * MOBA_WIDTH), row(2 * D_MODEL)],
        out_shape=[jax.ShapeDtypeStruct((m_rows, 2 * MLSTM_WIDTH), BF16),
                   jax.ShapeDtypeStruct((m_rows, 2 * MLSTM_WIDTH), BF16),
                   jax.ShapeDtypeStruct((m_rows, LANES), F32),
                   jax.ShapeDtypeStruct((m_rows, 3 * MOBA_WIDTH), BF16),
                   jax.ShapeDtypeStruct((m_rows, 2 * D_MODEL), BF16)],
        scratch_shapes=[pltpu.VMEM((n_chunks, tm + SUBLANES, PROJ_CHUNK), F32)],
        compiler_params=pltpu.CompilerParams(dimension_semantics=("arbitrary",),
                                             vmem_limit_bytes=VMEM_LIMIT_BYTES),
        name="in_proj",
    )(x2d, wqk, bqk, conv_qk, wvo, bvo, wif, bif, wb, bb, wg, bg)


def _log_sigmoid(x):
    return jnp.minimum(x, 0.0) - jnp.log1p(jnp.exp(-jnp.abs(x)))


def _mlstm_kernel(q_ref, k_ref, v_ref, o_ref, ift_ref, gain_ref, ya_ref, c_ref, n_ref, m_ref):
    h = pl.program_id(1)
    c = pl.program_id(2)
    L = MLSTM_CHUNK

    @pl.when(c == 0)
    def _():
        c_ref[...] = jnp.zeros_like(c_ref)
        n_ref[...] = jnp.zeros_like(n_ref)
        m_ref[...] = jnp.zeros_like(m_ref)

    rows = ift_ref[0]
    lane = lax.broadcasted_iota(jnp.int32, rows.shape, 1)
    sub = lax.broadcasted_iota(jnp.int32, rows.shape, 0)
    cum = _log_sigmoid(rows)
    shift = 1
    while shift < L:
        cum = cum + jnp.where(lane >= shift, pltpu.roll(cum, shift, axis=1), 0.0)
        shift *= 2
    li_row = jnp.sum(jnp.where(sub == h, rows, 0.0), axis=0, keepdims=True)
    b_row = jnp.sum(jnp.where(sub == h + MLSTM_HEADS, cum, 0.0), axis=0, keepdims=True)

    ri = lax.broadcasted_iota(jnp.int32, (L, L), 0)
    ci = lax.broadcasted_iota(jnp.int32, (L, L), 1)
    eye = ri == ci
    b_col = jnp.sum(jnp.where(eye, b_row, 0.0), axis=1, keepdims=True)
    li_col = jnp.sum(jnp.where(eye, li_row, 0.0), axis=1, keepdims=True)

    m_prev = m_ref[...]
    d_intra = jnp.where(ri >= ci, b_col - b_row + li_row, -jnp.inf)
    inter = b_col + m_prev
    m_q = jnp.maximum(inter, jnp.max(d_intra, axis=1, keepdims=True))
    w_inter = jnp.exp(inter - m_q)

    qb = q_ref[...]
    kb = k_ref[...]
    vb = v_ref[...]
    s = _dot_nt(qb, kb) * jnp.exp(d_intra - m_q)
    num = w_inter * _dot(qb, c_ref[...].astype(BF16)) + _dot(s.astype(BF16), vb)
    qn = jnp.sum(qb.astype(F32) * n_ref[...], axis=1, keepdims=True)
    den = w_inter * qn + jnp.sum(s, axis=1, keepdims=True)
    hh = num / jnp.maximum(jnp.abs(den), jnp.exp(-m_q))

    mu = jnp.mean(hh, axis=1, keepdims=True)
    hc = hh - mu
    var = jnp.mean(hc * hc, axis=1, keepdims=True)
    hn = hc * lax.rsqrt(var + LN_EPS) * gain_ref[...]
    ya_ref[...] = (jax.nn.sigmoid(o_ref[...].astype(F32)) * hn).astype(ya_ref.dtype)

    b_last = b_row[:, L - 1:L]
    d_state_row = b_last - b_row + li_row
    m_new = jnp.maximum(b_last + m_prev, jnp.max(d_state_row, axis=1, keepdims=True))
    w_prev = jnp.exp(b_last + m_prev - m_new)
    w_k = jnp.exp(b_last - b_col + li_col - m_new)
    kw = kb.astype(F32) * w_k
    c_ref[...] = w_prev * c_ref[...] + _dot(kw.T.astype(BF16), vb)
    n_ref[...] = w_prev * n_ref[...] + jnp.sum(kw, axis=0, keepdims=True)
    m_ref[...] = m_new


def _mlstm(qk, vo, ift, gain, batch, seq_len):
    L = MLSTM_CHUNK
    dh = MLSTM_HEAD_DIM
    nh = MLSTM_HEADS
    nc = seq_len // L
    m_rows = qk.shape[0]
    blk = lambda off: pl.BlockSpec((L, dh), lambda b, h, c: (b * nc + c, off + h))
    return pl.pallas_call(
        _mlstm_kernel,
        grid=(batch, nh, nc),
        in_specs=[blk(0), blk(nh), blk(0), blk(nh),
                  pl.BlockSpec((1, 2 * nh, L), lambda b, h, c: (b, 0, c)),
                  pl.BlockSpec((1, dh), lambda b, h, c: (0, h))],
        out_specs=blk(0),
        out_shape=jax.ShapeDtypeStruct((m_rows, MLSTM_WIDTH), BF16),
        scratch_shapes=[pltpu.VMEM((dh, dh), F32), pltpu.VMEM((1, dh), F32), pltpu.VMEM((1, 1), F32)],
        compiler_params=pltpu.CompilerParams(dimension_semantics=("arbitrary", "arbitrary", "arbitrary"),
                                             vmem_limit_bytes=VMEM_LIMIT_BYTES),
        name="mlstm",
    )(qk, qk, vo, vo, ift, gain)


def _t5_bucket(rel):
    n = jnp.maximum(rel, 0)
    max_exact = NUM_BUCKETS // 2
    nf = jnp.maximum(n, max_exact).astype(F32)
    large = max_exact + (jnp.log(nf / max_exact) / math.log(REL_MAX_DISTANCE / max_exact)
                         * (NUM_BUCKETS - max_exact)).astype(jnp.int32)
    large = jnp.minimum(large, NUM_BUCKETS - 1)
    return jnp.where(n < max_exact, n, large)


def _moba_kernel(rb_ref, q_ref, k_ref, vt_ref, bkt_own_ref, bkt_prev_ref, yb_ref,
                 kmean_ref, bias_own_ref, bias_prev_ref, mask_ref, m_ref, l_ref, acc_ref, *, nb):
    h = pl.program_id(1)
    i = pl.program_id(2)
    blk = MOBA_BLOCK

    @pl.when(i == 0)
    def _():
        def mean_body(j, carry):
            kj = k_ref[0, pl.ds(pl.multiple_of(j * blk, blk), blk), :].astype(F32)
            kmean_ref[pl.ds(j, 1), :] = jnp.sum(kj, axis=0, keepdims=True) * (1.0 / blk)
            return carry
        lax.fori_loop(0, nb, mean_body, 0)
        bo = bkt_own_ref[...]
        bp = bkt_prev_ref[...]
        own = jnp.full(bo.shape, NEG, F32)
        prev = jnp.zeros(bp.shape, F32)
        for bucket in range(NUM_BUCKETS):
            val = rb_ref[bucket, h]
            own = jnp.where(bo == bucket, val, own)
            prev = jnp.where(bp == bucket, val, prev)
        bias_own_ref[...] = own
        bias_prev_ref[...] = prev

    qb = q_ref[...]

    km = kmean_ref[...]
    km_hi = km.astype(BF16)
    r1 = km - km_hi.astype(F32)
    km_mid = r1.astype(BF16)
    km_lo = (r1 - km_mid.astype(F32)).astype(BF16)
    gate = _dot_nt(km_hi, qb) + _dot_nt(km_mid, qb) + _dot_nt(km_lo, qb)
    row = lax.broadcasted_iota(jnp.int32, gate.shape, 0)
    gate = jnp.where(row < i, gate, -jnp.inf)
    chosen = jnp.zeros(gate.shape, jnp.bool_)
    for r in range(MOBA_TOPK):
        mx = jnp.max(gate, axis=0, keepdims=True)
        first = jnp.min(jnp.where(gate == mx, row, nb), axis=0, keepdims=True)
        pick = row == first + jnp.where(i > r, 0, 2 * nb)
        chosen = jnp.logical_or(chosen, pick)
        gate = jnp.where(pick, -jnp.inf, gate)
    mask_ref[...] = jnp.where(chosen, 0.0, NEG)

    m_ref[...] = jnp.full(m_ref.shape, NEG, F32)
    l_ref[...] = jnp.zeros_like(l_ref)
    acc_ref[...] = jnp.zeros_like(acc_ref)

    def tile(j, add):
        kj = k_ref[0, pl.ds(pl.multiple_of(j * blk, blk), blk), :]
        s = _dot_nt(kj, qb) + add
        m_old = m_ref[...]
        m_new = jnp.maximum(m_old, jnp.max(s, axis=0, keepdims=True))
        alpha = jnp.exp(m_old - m_new)
        p = jnp.exp(s - m_new)
        l_ref[...] = alpha * l_ref[...] + jnp.sum(p, axis=0, keepdims=True)
        acc_ref[...] = alpha * acc_ref[...] + _dot(vt_ref[0, 0, j], p.astype(BF16))
        m_ref[...] = m_new

    far_bias = rb_ref[NUM_BUCKETS - 1, h]

    def far_body(j, carry):
        tile(j, mask_ref[pl.ds(j, 1), :] + far_bias)
        return carry
    lax.fori_loop(0, jnp.maximum(i - 1, 0), far_body, 0)

    @pl.when(i > 0)
    def _():
        tile(i - 1, bias_prev_ref[...] + mask_ref[pl.ds(i - 1, 1), :])

    tile(i, bias_own_ref[...])

    out_t = acc_ref[...] / l_ref[...]
    yb_ref[...] = out_t.T.astype(yb_ref.dtype)


def _moba(qkvb, rel_bias, batch, seq_len):
    blk = MOBA_BLOCK
    dh = MOBA_HEAD_DIM
    nh = MOBA_HEADS
    nb = seq_len // blk
    m_rows = qkvb.shape[0]
    qkvb3 = qkvb.reshape(batch, seq_len, 3 * MOBA_WIDTH)
    vt = qkvb3[:, :, 2 * MOBA_WIDTH:].reshape(batch, nb, blk, nh, dh).transpose(0, 3, 1, 4, 2)
    pos = jnp.arange(blk, dtype=jnp.int32)
    rel_own = pos[None, :] - pos[:, None]
    bkt_own = jnp.where(rel_own >= 0, _t5_bucket(rel_own), -1)
    bkt_prev = _t5_bucket(rel_own + blk)
    full2 = lambda shape: pl.BlockSpec(shape, lambda b, h, i: (0, 0))
    return pl.pallas_call(
        functools.partial(_moba_kernel, nb=nb),
        grid=(batch, nh, nb),
        in_specs=[pl.BlockSpec(memory_space=pltpu.SMEM),
                  pl.BlockSpec((blk, dh), lambda b, h, i: (b * nb + i, h)),
                  pl.BlockSpec((1, seq_len, dh), lambda b, h, i: (b, 0, nh + h)),
                  pl.BlockSpec((1, 1, nb, dh, blk), lambda b, h, i: (b, h, 0, 0, 0)),
                  full2((blk, blk)), full2((blk, blk))],
        out_specs=pl.BlockSpec((blk, dh), lambda b, h, i: (b * nb + i, h)),
        out_shape=jax.ShapeDtypeStruct((m_rows, MOBA_WIDTH), BF16),
        scratch_shapes=[pltpu.VMEM((nb, dh), F32),
                        pltpu.VMEM((blk, blk), F32), pltpu.VMEM((blk, blk), F32),
                        pltpu.VMEM((nb, blk), F32),
                        pltpu.VMEM((1, blk), F32), pltpu.VMEM((1, blk), F32), pltpu.VMEM((dh, blk), F32)],
        compiler_params=pltpu.CompilerParams(dimension_semantics=("arbitrary", "arbitrary", "arbitrary"),
                                             vmem_limit_bytes=VMEM_LIMIT_BYTES),
        name="moba",
    )(rel_bias, qkvb, qkvb3, vt, bkt_own, bkt_prev)


def _mixtail_kernel(x_ref, ya_ref, yb_ref, ga_ref, gb_ref, wa_ref, wb_ref, wo_ref, g_ref, b_ref, out_ref):
    pa = _dot(ya_ref[...], wa_ref[...])
    pb = _dot(yb_ref[...], wb_ref[...])
    merged = (jax.nn.sigmoid(ga_ref[...].astype(F32)) * pa
              + jax.nn.sigmoid(gb_ref[...].astype(F32)) * pb)
    mix = _dot(merged.astype(BF16), wo_ref[...])
    z = ALPHA * x_ref[...] + mix
    out_ref[...] = _layer_norm_rows(z, g_ref[...], b_ref[...])


def _mix_tail(x2d, ya, yb, gg, w_a, w_b, w_o, ln_g, ln_b, tm=512):
    m_rows = x2d.shape[0]
    row = lambda j: pl.BlockSpec((tm, D_MODEL), lambda i: (i, j))
    wa = w_a.astype(BF16)
    wb = w_b.astype(BF16)
    wo = w_o.astype(BF16)
    return pl.pallas_call(
        _mixtail_kernel,
        grid=(m_rows // tm,),
        in_specs=[row(0), row(0), row(0), row(0), row(1),
                  _resident(wa.shape), _resident(wb.shape), _resident(wo.shape),
                  _resident((1, D_MODEL)), _resident((1, D_MODEL))],
        out_specs=row(0),
        out_shape=jax.ShapeDtypeStruct((m_rows, D_MODEL), F32),
        compiler_params=pltpu.CompilerParams(dimension_semantics=("arbitrary",),
                                             vmem_limit_bytes=VMEM_LIMIT_BYTES),
        name="mix_tail",
    )(x2d, ya, yb, gg, gg, wa, wb, wo, ln_g[None, :], ln_b[None, :])


def _ffn_kernel(x_ref, wup_ref, cw_ref, wdn_ref, g_ref, b_ref, out_ref,
                xb_ref, ubuf_ref, carry_ref, acc_ref, *, tm, tiles_per_seq):
    i = pl.program_id(0)
    seq_start = (i % tiles_per_seq) == 0
    halo = FFN_CONV - 1
    xb_ref[...] = x_ref[...].astype(BF16)
    acc_ref[...] = jnp.zeros_like(acc_ref)

    def body(c, carry):
        u = _dot(xb_ref[...], wup_ref[c])

        @pl.when(seq_start)
        def _():
            ubuf_ref[0:SUBLANES, :] = jnp.zeros((SUBLANES, 2 * FFN_CHUNK), F32)

        @pl.when(jnp.logical_not(seq_start))
        def _():
            ubuf_ref[0:SUBLANES, :] = carry_ref[c]

        ubuf_ref[SUBLANES:SUBLANES + tm, :] = u
        carry_ref[c] = u[tm - SUBLANES:tm, :]
        cw = cw_ref[c]
        conv = cw[halo:halo + 1, :] * u
        for k in range(halo):
            off = SUBLANES - halo + k
            conv = conv + cw[k:k + 1, :] * ubuf_ref[off:off + tm, :]
        gate = conv[:, :FFN_CHUNK]
        up = conv[:, FFN_CHUNK:]
        act = (gate * jax.nn.sigmoid(gate) * up).astype(BF16)
        acc_ref[...] += _dot(act, wdn_ref[c])
        return carry
    lax.fori_loop(0, N_FFN_CHUNKS, body, 0)

    z = ALPHA * x_ref[...] + acc_ref[...]
    out_ref[...] = _layer_norm_rows(z, g_ref[...], b_ref[...])


def _ffn(x2d, w_up, conv_ffn, w_down, ln_g, ln_b, seq_len, tm=512):
    m_rows = x2d.shape[0]
    nch = N_FFN_CHUNKS
    fc = FFN_CHUNK

    def pair(t):
        lead = t.shape[:-1]
        t = t.reshape(*lead, 2, nch, fc)
        t = jnp.moveaxis(t, -2, 0)
        return t.reshape(nch, *lead, 2 * fc)

    wup = pair(w_up).astype(BF16)
    cw = pair(conv_ffn)
    wdn = w_down.reshape(nch, fc, D_MODEL).astype(BF16)
    row = pl.BlockSpec((tm, D_MODEL), lambda i: (i, 0))
    return pl.pallas_call(
        functools.partial(_ffn_kernel, tm=tm, tiles_per_seq=seq_len // tm),
        grid=(m_rows // tm,),
        in_specs=[row, _resident(wup.shape), _resident(cw.shape), _resident(wdn.shape),
                  _resident((1, D_MODEL)), _resident((1, D_MODEL))],
        out_specs=row,
        out_shape=jax.ShapeDtypeStruct((m_rows, D_MODEL), F32),
        scratch_shapes=[pltpu.VMEM((tm, D_MODEL), BF16),
                        pltpu.VMEM((tm + SUBLANES, 2 * fc), F32),
                        pltpu.VMEM((nch, SUBLANES, 2 * fc), F32),
                        pltpu.VMEM((tm, D_MODEL), F32)],
        compiler_params=pltpu.CompilerParams(dimension_semantics=("arbitrary",),
                                             vmem_limit_bytes=VMEM_LIMIT_BYTES),
        name="ffn",
    )(x2d, wup, cw, wdn, ln_g[None, :], ln_b[None, :])


def kernel(x, w_in, b_in, conv_qk, mlstm_norm, rel_bias, w_branch_a, w_branch_b, w_out,
           ln1_g, ln1_b, w_up, conv_ffn, w_down, ln2_g, ln2_b):
    batch, seq_len, d_model = x.shape
    assert d_model == D_MODEL
    assert seq_len % 512 == 0
    h = x.reshape(batch * seq_len, d_model)
    for l in range(DEPTH):
        qk, vo, ifp, qkvb, gg = _in_proj(h, w_in[l], b_in[l], conv_qk[l], seq_len)
        ift = ifp[:, :2 * MLSTM_HEADS].reshape(batch, seq_len, 2 * MLSTM_HEADS).transpose(0, 2, 1)
        ya = _mlstm(qk, vo, ift, mlstm_norm[l][None, :], batch, seq_len)
        yb = _moba(qkvb, rel_bias, batch, seq_len)
        h = _mix_tail(h, ya, yb, gg, w_branch_a[l], w_branch_b[l], w_out[l], ln1_g[l], ln1_b[l])
        h = _ffn(h, w_up[l], conv_ffn[l], w_down[l], ln2_g[l], ln2_b[l], seq_len)
    return h.reshape(batch, seq_len, d_model)
```

```python
import functools
import math

import jax
import jax.numpy as jnp
from jax import lax
from jax.experimental import pallas as pl
from jax.experimental.pallas import tpu as pltpu

D_MODEL = 1024
DEPTH = 2
MLSTM_HEADS = 4
MLSTM_HEAD_DIM = 256
MLSTM_WIDTH = MLSTM_HEADS * MLSTM_HEAD_DIM
MLSTM_CONV = 4
MOBA_HEADS = 8
MOBA_HEAD_DIM = 128
MOBA_WIDTH = MOBA_HEADS * MOBA_HEAD_DIM
MOBA_BLOCK = 256
MOBA_TOPK = 3
NUM_BUCKETS = 32
REL_MAX_DISTANCE = 128
D_FF = 2816
FFN_CONV = 3
ALPHA = (2 * DEPTH) ** 0.25
LN_EPS = 1e-5

SUBLANES = 8
LANES = 128
VMEM_LIMIT_BYTES = 56 * 1024 * 1024

MLSTM_CHUNK = 256
FFN_CHUNK = 256
N_FFN_CHUNKS = D_FF // FFN_CHUNK
PROJ_CHUNK = 512
NEG = -1e30
MOBA_GROUP = 4
MASK_PAD = SUBLANES

BF16 = jnp.bfloat16
F32 = jnp.float32


def _dot(a, b):
    return jnp.dot(a, b, preferred_element_type=F32)


def _dot_nt(a, b):
    return lax.dot_general(a, b, (((1,), (1,)), ((), ())), preferred_element_type=F32)


def _resident(shape):
    nd = len(shape)
    return pl.BlockSpec(shape, lambda *_: (0,) * nd, pipeline_mode=pl.Buffered(1))


def _layer_norm_rows(z, g, b):
    mu = jnp.mean(z, axis=-1, keepdims=True)
    zc = z - mu
    var = jnp.mean(zc * zc, axis=-1, keepdims=True)
    return zc * lax.rsqrt(var + LN_EPS) * g + b


def _inproj_kernel(x_ref, wqk_ref, bqk_ref, cw_ref, wvo_ref, bvo_ref, wif_ref, bif_ref,
                   wb_ref, bb_ref, wg_ref, bg_ref,
                   qk_ref, vo_ref, if_ref, qkvb_ref, gg_ref, *ubuf_refs, tm, tiles_per_seq):
    i = pl.program_id(0)
    xb = x_ref[...].astype(BF16)
    halo = MLSTM_CONV - 1

    @pl.when(i == 0)
    def _():
        for ubuf_ref in ubuf_refs:
            ubuf_ref[tm:tm + SUBLANES, :] = jnp.zeros((SUBLANES, PROJ_CHUNK), F32)

    at_start = jnp.full((SUBLANES, PROJ_CHUNK), i % tiles_per_seq, jnp.int32) == 0
    for c, ubuf_ref in enumerate(ubuf_refs):
        cs = slice(c * PROJ_CHUNK, (c + 1) * PROJ_CHUNK)
        u = _dot(xb, wqk_ref[:, cs]) + bqk_ref[:, cs]
        ubuf_ref[0:SUBLANES, :] = jnp.where(at_start, 0.0, ubuf_ref[tm:tm + SUBLANES, :])
        ubuf_ref[SUBLANES:SUBLANES + tm, :] = u
        conv = cw_ref[halo:halo + 1, cs] * u
        for k in range(halo):
            off = SUBLANES - halo + k
            conv = conv + cw_ref[k:k + 1, cs] * ubuf_ref[off:off + tm, :]
        act = conv * jax.nn.sigmoid(conv)
        if c * PROJ_CHUNK >= MLSTM_WIDTH:
            act = act * (MLSTM_HEAD_DIM ** -0.5)
        qk_ref[:, cs] = act.astype(qk_ref.dtype)

    for c in range(2 * MLSTM_WIDTH // PROJ_CHUNK):
        cs = slice(c * PROJ_CHUNK, (c + 1) * PROJ_CHUNK)
        vo_ref[:, cs] = (_dot(xb, wvo_ref[:, cs]) + bvo_ref[:, cs]).astype(vo_ref.dtype)
    if_ref[...] = _dot(xb, wif_ref[...]) + bif_ref[...]
    for c in range(3 * MOBA_WIDTH // PROJ_CHUNK):
        cs = slice(c * PROJ_CHUNK, (c + 1) * PROJ_CHUNK)
        qkvb_ref[:, cs] = (_dot(xb, wb_ref[:, cs]) + bb_ref[:, cs]).astype(qkvb_ref.dtype)
    for c in range(2 * D_MODEL // PROJ_CHUNK):
        cs = slice(c * PROJ_CHUNK, (c + 1) * PROJ_CHUNK)
        gg_ref[:, cs] = (_dot(xb, wg_ref[:, cs]) + bg_ref[:, cs]).astype(gg_ref.dtype)


def _in_proj(x2d, w_in, b_in, conv_qk, seq_len, tm=256):
    m_rows = x2d.shape[0]
    w4 = 4 * MLSTM_WIDTH
    nh = MLSTM_HEADS
    b0 = w4 + 2 * nh
    scale_b = MOBA_HEAD_DIM ** -0.5
    col_scale = jnp.concatenate([jnp.full((MOBA_WIDTH,), scale_b, F32), jnp.ones((2 * MOBA_WIDTH,), F32)])
    wqk = w_in[:, :2 * MLSTM_WIDTH].astype(BF16)
    bqk = b_in[None, :2 * MLSTM_WIDTH]
    wvo = w_in[:, 2 * MLSTM_WIDTH:w4].astype(BF16)
    bvo = b_in[None, 2 * MLSTM_WIDTH:w4]
    wif = jnp.pad(w_in[:, w4:b0], ((0, 0), (0, LANES - 2 * nh))).astype(BF16)
    bif = jnp.pad(b_in[w4:b0], (0, LANES - 2 * nh))[None, :]
    wb = (w_in[:, b0:b0 + 3 * MOBA_WIDTH] * col_scale).astype(BF16)
    bb = (b_in[b0:b0 + 3 * MOBA_WIDTH] * col_scale)[None, :]
    wg = w_in[:, b0 + 3 * MOBA_WIDTH:].astype(BF16)
    bg = b_in[None, b0 + 3 * MOBA_WIDTH:]

    row = lambda n: pl.BlockSpec((tm, n), lambda i: (i, 0))
    n_chunks = 2 * MLSTM_WIDTH // PROJ_CHUNK
    return pl.pallas_call(
        functools.partial(_inproj_kernel, tm=tm, tiles_per_seq=seq_len // tm),
        grid=(m_rows // tm,),
        in_specs=[row(D_MODEL),
                  _resident(wqk.shape), _resident(bqk.shape), _resident(conv_qk.shape),
                  _resident(wvo.shape), _resident(bvo.shape),
                  _resident(wif.shape), _resident(bif.shape),
                  _resident(wb.shape), _resident(bb.shape),
                  _resident(wg.shape), _resident(bg.shape)],
        out_specs=[row(2 * MLSTM_WIDTH), row(2 * MLSTM_WIDTH), row(LANES), row(3 * MOBA_WIDTH), row(2 * D_MODEL)],
        out_shape=[jax.ShapeDtypeStruct((m_rows, 2 * MLSTM_WIDTH), BF16),
                   jax.ShapeDtypeStruct((m_rows, 2 * MLSTM_WIDTH), BF16),
                   jax.ShapeDtypeStruct((m_rows, LANES), F32),
                   jax.ShapeDtypeStruct((m_rows, 3 * MOBA_WIDTH), BF16),
                   jax.ShapeDtypeStruct((m_rows, 2 * D_MODEL), BF16)],
        scratch_shapes=[pltpu.VMEM((tm + SUBLANES, PROJ_CHUNK), F32)] * n_chunks,
        compiler_params=pltpu.CompilerParams(dimension_semantics=("arbitrary",),
                                             vmem_limit_bytes=VMEM_LIMIT_BYTES),
        name="in_proj",
    )(x2d, wqk, bqk, conv_qk, wvo, bvo, wif, bif, wb, bb, wg, bg)


def _log_sigmoid(x):
    return jnp.minimum(x, 0.0) - jnp.log1p(jnp.exp(-jnp.abs(x)))


def _mlstm_kernel(q_ref, k_ref, v_ref, o_ref, ift_ref, gain_ref, ya_ref, c_ref, n_ref, m_ref):
    h = pl.program_id(1)
    c = pl.program_id(2)
    L = MLSTM_CHUNK

    @pl.when(c == 0)
    def _():
        c_ref[...] = jnp.zeros_like(c_ref)
        n_ref[...] = jnp.zeros_like(n_ref)
        m_ref[...] = jnp.zeros_like(m_ref)

    rows = ift_ref[0]
    lane = lax.broadcasted_iota(jnp.int32, rows.shape, 1)
    sub = lax.broadcasted_iota(jnp.int32, rows.shape, 0)
    cum = _log_sigmoid(rows)
    shift = 1
    while shift < L:
        cum = cum + jnp.where(lane >= shift, pltpu.roll(cum, shift, axis=1), 0.0)
        shift *= 2
    li_row = jnp.sum(jnp.where(sub == h, rows, 0.0), axis=0, keepdims=True)
    b_row = jnp.sum(jnp.where(sub == h + MLSTM_HEADS, cum, 0.0), axis=0, keepdims=True)

    ri = lax.broadcasted_iota(jnp.int32, (L, L), 0)
    ci = lax.broadcasted_iota(jnp.int32, (L, L), 1)
    eye = ri == ci
    b_col = jnp.sum(jnp.where(eye, b_row, 0.0), axis=1, keepdims=True)
    li_col = jnp.sum(jnp.where(eye, li_row, 0.0), axis=1, keepdims=True)

    m_prev = m_ref[...]
    d_intra = jnp.where(ri >= ci, b_col - b_row + li_row, -jnp.inf)
    inter = b_col + m_prev
    m_q = jnp.maximum(inter, jnp.max(d_intra, axis=1, keepdims=True))
    w_inter = jnp.exp(inter - m_q)

    qb = q_ref[...]
    kb = k_ref[...]
    vb = v_ref[...]
    s = _dot_nt(qb, kb) * jnp.exp(d_intra - m_q)
    num = w_inter * _dot(qb, c_ref[...].astype(BF16)) + _dot(s.astype(BF16), vb)
    qn = jnp.sum(qb.astype(F32) * n_ref[...], axis=1, keepdims=True)
    den = w_inter * qn + jnp.sum(s, axis=1, keepdims=True)
    hh = num / jnp.maximum(jnp.abs(den), jnp.exp(-m_q))

    mu = jnp.mean(hh, axis=1, keepdims=True)
    hc = hh - mu
    var = jnp.mean(hc * hc, axis=1, keepdims=True)
    hn = hc * lax.rsqrt(var + LN_EPS) * gain_ref[...]
    ya_ref[...] = (jax.nn.sigmoid(o_ref[...].astype(F32)) * hn).astype(ya_ref.dtype)

    b_last = b_row[:, L - 1:L]
    d_state_row = b_last - b_row + li_row
    m_new = jnp.maximum(b_last + m_prev, jnp.max(d_state_row, axis=1, keepdims=True))
    w_prev = jnp.exp(b_last + m_prev - m_new)
    w_k = jnp.exp(b_last - b_col + li_col - m_new)
    kw = kb.astype(F32) * w_k
    c_ref[...] = w_prev * c_ref[...] + _dot(kw.T.astype(BF16), vb)
    n_ref[...] = w_prev * n_ref[...] + jnp.sum(kw, axis=0, keepdims=True)
    m_ref[...] = m_new


def _mlstm(qk, vo, ift, gain, batch, seq_len):
    L = MLSTM_CHUNK
    dh = MLSTM_HEAD_DIM
    nh = MLSTM_HEADS
    nc = seq_len // L
    m_rows = qk.shape[0]
    blk = lambda off: pl.BlockSpec((L, dh), lambda b, h, c: (b * nc + c, off + h))
    return pl.pallas_call(
        _mlstm_kernel,
        grid=(batch, nh, nc),
        in_specs=[blk(0), blk(nh), blk(0), blk(nh),
                  pl.BlockSpec((1, 2 * nh, L), lambda b, h, c: (b, 0, c)),
                  pl.BlockSpec((1, dh), lambda b, h, c: (0, h))],
        out_specs=blk(0),
        out_shape=jax.ShapeDtypeStruct((m_rows, MLSTM_WIDTH), BF16),
        scratch_shapes=[pltpu.VMEM((dh, dh), F32), pltpu.VMEM((1, dh), F32), pltpu.VMEM((1, 1), F32)],
        compiler_params=pltpu.CompilerParams(dimension_semantics=("arbitrary", "arbitrary", "arbitrary"),
                                             vmem_limit_bytes=VMEM_LIMIT_BYTES),
        name="mlstm",
    )(qk, qk, vo, vo, ift, gain)


def _t5_bucket(rel):
    n = jnp.maximum(rel, 0)
    max_exact = NUM_BUCKETS // 2
    nf = jnp.maximum(n, max_exact).astype(F32)
    large = max_exact + (jnp.log(nf / max_exact) / math.log(REL_MAX_DISTANCE / max_exact)
                         * (NUM_BUCKETS - max_exact)).astype(jnp.int32)
    large = jnp.minimum(large, NUM_BUCKETS - 1)
    return jnp.where(n < max_exact, n, large)


def _moba_kernel(rb_ref, q_ref, k_ref, vt_ref, bkt_own_ref, bkt_prev_ref, yb_ref,
                 kmean_ref, bias_own_ref, bias_prev_ref, mask_ref, sa_ref, sb_ref, m_ref, l_ref, acc_ref, *, nb):
    h = pl.program_id(1)
    i = pl.program_id(2)
    blk = MOBA_BLOCK

    @pl.when(i == 0)
    def _():
        def mean_body(j, carry):
            kj = k_ref[0, pl.ds(pl.multiple_of(j * blk, blk), blk), :].astype(F32)
            kmean_ref[pl.ds(j, 1), :] = jnp.sum(kj, axis=0, keepdims=True) * (1.0 / blk)
            return carry
        lax.fori_loop(0, nb, mean_body, 0)
        bo = bkt_own_ref[...]
        bp = bkt_prev_ref[...]
        own = jnp.full(bo.shape, NEG, F32)
        prev = jnp.zeros(bp.shape, F32)
        for bucket in range(NUM_BUCKETS):
            val = rb_ref[bucket, h]
            own = jnp.where(bo == bucket, val, own)
            prev = jnp.where(bp == bucket, val, prev)
        bias_own_ref[...] = own
        bias_prev_ref[...] = prev

    qb = q_ref[...]

    km = kmean_ref[...]
    km_hi = km.astype(BF16)
    r1 = km - km_hi.astype(F32)
    km_mid = r1.astype(BF16)
    km_lo = (r1 - km_mid.astype(F32)).astype(BF16)
    gate = _dot_nt(km_hi, qb) + _dot_nt(km_mid, qb) + _dot_nt(km_lo, qb)
    row = lax.broadcasted_iota(jnp.int32, gate.shape, 0)
    gate = jnp.where(row < i, gate, -jnp.inf)
    chosen = jnp.zeros(gate.shape, jnp.bool_)
    for r in range(MOBA_TOPK):
        mx = jnp.max(gate, axis=0, keepdims=True)
        first = jnp.min(jnp.where(gate == mx, row, nb), axis=0, keepdims=True)
        pick = row == first + jnp.where(i > r, 0, 2 * nb)
        chosen = jnp.logical_or(chosen, pick)
        gate = jnp.where(pick, -jnp.inf, gate)
    mask_ref[0:MASK_PAD, :] = jnp.full((MASK_PAD, blk), NEG, F32)
    mask_ref[MASK_PAD:MASK_PAD + nb, :] = jnp.where(chosen, 0.0, NEG)

    G = MOBA_GROUP
    far_bias = rb_ref[NUM_BUCKETS - 1, h]

    def mask_row(j):
        return mask_ref[pl.ds(j + MASK_PAD, 1), :]

    def logits_group(t, s_ref):
        for r in range(G):
            jc = jnp.maximum(i - G * t - r, 0)
            kj = k_ref[0, pl.ds(pl.multiple_of(jc * blk, blk), blk), :]
            s_ref[r * blk:(r + 1) * blk, :] = _dot_nt(kj, qb)

    def softmax_pv(t, s_ref, rows, m_old):
        cmax = [jnp.max(s_ref[r * blk:(r + 1) * blk, :], axis=0, keepdims=True) + rows[r] for r in range(G)]
        m_new = functools.reduce(jnp.maximum, cmax)
        if m_old is not None:
            m_new = jnp.maximum(m_new, m_old)
        lsum = jnp.zeros((1, blk), F32)
        pv = jnp.zeros((MOBA_HEAD_DIM, blk), F32)
        for r in range(G):
            p = jnp.exp(s_ref[r * blk:(r + 1) * blk, :] - (m_new - rows[r]))
            lsum = lsum + jnp.sum(p, axis=0, keepdims=True)
            jc = jnp.maximum(i - G * t - r, 0)
            pv = pv + _dot(vt_ref[0, 0, jc], p.astype(BF16))
        return m_new, lsum, pv

    def far_step(t, s_ref, next_ref):
        if next_ref is not None:
            logits_group(t + 1, next_ref)
        rows = [mask_row(i - G * t - r) + far_bias for r in range(G)]
        m_old = m_ref[...]
        m_new, lsum, pv = softmax_pv(t, s_ref, rows, m_old)
        alpha = jnp.exp(m_old - m_new)
        l_ref[...] = alpha * l_ref[...] + lsum
        acc_ref[...] = alpha * acc_ref[...] + pv
        m_ref[...] = m_new

    logits_group(0, sa_ref)
    logits_group(1, sb_ref)
    sa_ref[0:blk, :] = sa_ref[0:blk, :] + bias_own_ref[...]
    sa_ref[blk:2 * blk, :] = sa_ref[blk:2 * blk, :] + bias_prev_ref[...]
    rows0 = [jnp.zeros((1, blk), F32), mask_row(i - 1)] + [mask_row(i - r) + far_bias for r in range(2, G)]
    m0, l0, pv0 = softmax_pv(0, sa_ref, rows0, None)
    m_ref[...] = m0
    l_ref[...] = l0
    acc_ref[...] = pv0

    n_far = (i + G) // G - 1

    def pair_body(u, carry):
        far_step(2 * u + 1, sb_ref, sa_ref)
        far_step(2 * u + 2, sa_ref, sb_ref)
        return carry
    lax.fori_loop(0, n_far // 2, pair_body, 0)

    @pl.when(n_far % 2 == 1)
    def _():
        far_step(n_far, sb_ref, None)

    out_t = acc_ref[...] / l_ref[...]
    yb_ref[...] = out_t.T.astype(yb_ref.dtype)


def _moba(qkvb, rel_bias, batch, seq_len):
    blk = MOBA_BLOCK
    dh = MOBA_HEAD_DIM
    nh = MOBA_HEADS
    nb = seq_len // blk
    m_rows = qkvb.shape[0]
    qkvb3 = qkvb.reshape(batch, seq_len, 3 * MOBA_WIDTH)
    vt = qkvb3[:, :, 2 * MOBA_WIDTH:].reshape(batch, nb, blk, nh, dh).transpose(0, 3, 1, 4, 2)
    pos = jnp.arange(blk, dtype=jnp.int32)
    rel_own = pos[None, :] - pos[:, None]
    bkt_own = jnp.where(rel_own >= 0, _t5_bucket(rel_own), -1)
    bkt_prev = _t5_bucket(rel_own + blk)
    full2 = lambda shape: pl.BlockSpec(shape, lambda b, h, i: (0, 0))
    return pl.pallas_call(
        functools.partial(_moba_kernel, nb=nb),
        grid=(batch, nh, nb),
        in_specs=[pl.BlockSpec(memory_space=pltpu.SMEM),
                  pl.BlockSpec((blk, dh), lambda b, h, i: (b * nb + i, h)),
                  pl.BlockSpec((1, seq_len, dh), lambda b, h, i: (b, 0, nh + h)),
                  pl.BlockSpec((1, 1, nb, dh, blk), lambda b, h, i: (b, h, 0, 0, 0)),
                  full2((blk, blk)), full2((blk, blk))],
        out_specs=pl.BlockSpec((blk, dh), lambda b, h, i: (b * nb + i, h)),
        out_shape=jax.ShapeDtypeStruct((m_rows, MOBA_WIDTH), BF16),
        scratch_shapes=[pltpu.VMEM((nb, dh), F32),
                        pltpu.VMEM((blk, blk), F32), pltpu.VMEM((blk, blk), F32),
                        pltpu.VMEM((MASK_PAD + nb, blk), F32),
                        pltpu.VMEM((MOBA_GROUP * blk, blk), F32), pltpu.VMEM((MOBA_GROUP * blk, blk), F32),
                        pltpu.VMEM((1, blk), F32), pltpu.VMEM((1, blk), F32), pltpu.VMEM((dh, blk), F32)],
        compiler_params=pltpu.CompilerParams(dimension_semantics=("arbitrary", "arbitrary", "arbitrary"),
                                             vmem_limit_bytes=VMEM_LIMIT_BYTES),
        name="moba",
    )(rel_bias, qkvb, qkvb3, vt, bkt_own, bkt_prev)


def _mixtail_kernel(x_ref, ya_ref, yb_ref, ga_ref, gb_ref, wa_ref, wb_ref, wo_ref, g_ref, b_ref, out_ref):
    pa = _dot(ya_ref[...], wa_ref[...])
    pb = _dot(yb_ref[...], wb_ref[...])
    merged = (jax.nn.sigmoid(ga_ref[...].astype(F32)) * pa
              + jax.nn.sigmoid(gb_ref[...].astype(F32)) * pb)
    mix = _dot(merged.astype(BF16), wo_ref[...])
    z = ALPHA * x_ref[...] + mix
    out_ref[...] = _layer_norm_rows(z, g_ref[...], b_ref[...])


def _mix_tail(x2d, ya, yb, gg, w_a, w_b, w_o, ln_g, ln_b, tm=512):
    m_rows = x2d.shape[0]
    row = lambda j: pl.BlockSpec((tm, D_MODEL), lambda i: (i, j))
    wa = w_a.astype(BF16)
    wb = w_b.astype(BF16)
    wo = w_o.astype(BF16)
    return pl.pallas_call(
        _mixtail_kernel,
        grid=(m_rows // tm,),
        in_specs=[row(0), row(0), row(0), row(0), row(1),
                  _resident(wa.shape), _resident(wb.shape), _resident(wo.shape),
                  _resident((1, D_MODEL)), _resident((1, D_MODEL))],
        out_specs=row(0),
        out_shape=jax.ShapeDtypeStruct((m_rows, D_MODEL), F32),
        compiler_params=pltpu.CompilerParams(dimension_semantics=("arbitrary",),
                                             vmem_limit_bytes=VMEM_LIMIT_BYTES),
        name="mix_tail",
    )(x2d, ya, yb, gg, gg, wa, wb, wo, ln_g[None, :], ln_b[None, :])


def _ffn_kernel(x_ref, wup_ref, cw_ref, wdn_ref, g_ref, b_ref, out_ref,
                xb_ref, ua_ref, ub_ref, carry_ref, acc_ref, *, tm, tiles_per_seq):
    i = pl.program_id(0)
    halo = FFN_CONV - 1
    xb_ref[...] = x_ref[...].astype(BF16)
    acc_ref[...] = jnp.zeros_like(acc_ref)

    @pl.when(i == 0)
    def _():
        carry_ref[...] = jnp.zeros_like(carry_ref)

    at_start = jnp.full((SUBLANES, 2 * FFN_CHUNK), i % tiles_per_seq, jnp.int32) == 0

    def up_proj(c, u_ref):
        u = _dot(xb_ref[...], wup_ref[c])
        u_ref[0:SUBLANES, :] = jnp.where(at_start, 0.0, carry_ref[c])
        u_ref[SUBLANES:SUBLANES + tm, :] = u
        carry_ref[c] = u[tm - SUBLANES:tm, :]

    def act_down(c, u_ref):
        cw = cw_ref[c]
        conv = cw[halo:halo + 1, :] * u_ref[SUBLANES:SUBLANES + tm, :]
        for k in range(halo):
            off = SUBLANES - halo + k
            conv = conv + cw[k:k + 1, :] * u_ref[off:off + tm, :]
        gate = conv[:, :FFN_CHUNK]
        up = conv[:, FFN_CHUNK:]
        act = (gate * jax.nn.sigmoid(gate) * up).astype(BF16)
        acc_ref[...] += _dot(act, wdn_ref[c])

    up_proj(0, ua_ref)

    def pair_body(p, carry):
        up_proj(2 * p + 1, ub_ref)
        act_down(2 * p, ua_ref)
        up_proj(2 * p + 2, ua_ref)
        act_down(2 * p + 1, ub_ref)
        return carry
    lax.fori_loop(0, N_FFN_CHUNKS // 2, pair_body, 0)
    act_down(N_FFN_CHUNKS - 1, ua_ref)

    z = ALPHA * x_ref[...] + acc_ref[...]
    out_ref[...] = _layer_norm_rows(z, g_ref[...], b_ref[...])


def _ffn(x2d, w_up, conv_ffn, w_down, ln_g, ln_b, seq_len, tm=512):
    m_rows = x2d.shape[0]
    nch = N_FFN_CHUNKS
    fc = FFN_CHUNK

    def pair(t):
        lead = t.shape[:-1]
        t = t.reshape(*lead, 2, nch, fc)
        t = jnp.moveaxis(t, -2, 0)
        return t.reshape(nch, *lead, 2 * fc)

    wup = pair(w_up).astype(BF16)
    cw = pair(conv_ffn)
    wdn = w_down.reshape(nch, fc, D_MODEL).astype(BF16)
    row = pl.BlockSpec((tm, D_MODEL), lambda i: (i, 0))
    return pl.pallas_call(
        functools.partial(_ffn_kernel, tm=tm, tiles_per_seq=seq_len // tm),
        grid=(m_rows // tm,),
        in_specs=[row, _resident(wup.shape), _resident(cw.shape), _resident(wdn.shape),
                  _resident((1, D_MODEL)), _resident((1, D_MODEL))],
        out_specs=row,
        out_shape=jax.ShapeDtypeStruct((m_rows, D_MODEL), F32),
        scratch_shapes=[pltpu.VMEM((tm, D_MODEL), BF16),
                        pltpu.VMEM((tm + SUBLANES, 2 * fc), F32), pltpu.VMEM((tm + SUBLANES, 2 * fc), F32),
                        pltpu.VMEM((nch, SUBLANES, 2 * fc), F32),
                        pltpu.VMEM((tm, D_MODEL), F32)],
        compiler_params=pltpu.CompilerParams(dimension_semantics=("arbitrary",),
                                             vmem_limit_bytes=VMEM_LIMIT_BYTES),
        name="ffn",
    )(x2d, wup, cw, wdn, ln_g[None, :], ln_b[None, :])


def kernel(x, w_in, b_in, conv_qk, mlstm_norm, rel_bias, w_branch_a, w_branch_b, w_out,
           ln1_g, ln1_b, w_up, conv_ffn, w_down, ln2_g, ln2_b):
    batch, seq_len, d_model = x.shape
    assert d_model == D_MODEL
    assert seq_len % 512 == 0
    h = x.reshape(batch * seq_len, d_model)
    for l in range(DEPTH):
        qk, vo, ifp, qkvb, gg = _in_proj(h, w_in[l], b_in[l], conv_qk[l], seq_len)
        ift = ifp[:, :2 * MLSTM_HEADS].reshape(batch, seq_len, 2 * MLSTM_HEADS).transpose(0, 2, 1)
        ya = _mlstm(qk, vo, ift, mlstm_norm[l][None, :], batch, seq_len)
        yb = _moba(qkvb, rel_bias, batch, seq_len)
        h = _mix_tail(h, ya, yb, gg, w_branch_a[l], w_branch_b[l], w_out[l], ln1_g[l], ln1_b[l])
        h = _ffn(h, w_up[l], conv_ffn[l], w_down[l], ln2_g[l], ln2_b[l], seq_len)
    return h.reshape(batch, seq_len, d_model)
```

```python
import functools
import math

import jax
import jax.numpy as jnp
from jax import lax
from jax.experimental import pallas as pl
from jax.experimental.pallas import tpu as pltpu

D_MODEL = 1024
DEPTH = 2
MLSTM_HEADS = 4
MLSTM_HEAD_DIM = 256
MLSTM_WIDTH = MLSTM_HEADS * MLSTM_HEAD_DIM
MLSTM_CONV = 4
MOBA_HEADS = 8
MOBA_HEAD_DIM = 128
MOBA_WIDTH = MOBA_HEADS * MOBA_HEAD_DIM
MOBA_BLOCK = 256
MOBA_TOPK = 3
NUM_BUCKETS = 32
REL_MAX_DISTANCE = 128
D_FF = 2816
FFN_CONV = 3
ALPHA = (2 * DEPTH) ** 0.25
LN_EPS = 1e-5

SUBLANES = 8
LANES = 128
VMEM_LIMIT_BYTES = 56 * 1024 * 1024

MLSTM_CHUNK = 256
FFN_CHUNK = 256
N_FFN_CHUNKS = D_FF // FFN_CHUNK
PROJ_CHUNK = 512
NEG = -1e30
MOBA_GROUP = 4
MOBA_QBLOCKS = 2
LOG2E = math.log2(math.e)
MASK_PAD = SUBLANES

BF16 = jnp.bfloat16
F32 = jnp.float32


def _dot(a, b):
    return jnp.dot(a, b, preferred_element_type=F32)


def _dot_nt(a, b):
    return lax.dot_general(a, b, (((1,), (1,)), ((), ())), preferred_element_type=F32)


def _dot_tn(a, b):
    return lax.dot_general(a, b, (((0,), (0,)), ((), ())), preferred_element_type=F32)


def _resident(shape):
    nd = len(shape)
    return pl.BlockSpec(shape, lambda *_: (0,) * nd, pipeline_mode=pl.Buffered(1))


def _layer_norm_rows(z, g, b):
    mu = jnp.mean(z, axis=-1, keepdims=True)
    zc = z - mu
    var = jnp.mean(zc * zc, axis=-1, keepdims=True)
    return zc * lax.rsqrt(var + LN_EPS) * g + b


def _inproj_kernel(x_ref, wqk_ref, bqk_ref, cw_ref, wvo_ref, bvo_ref, wif_ref, bif_ref,
                   wb_ref, bb_ref, wg_ref, bg_ref,
                   qk_ref, vo_ref, if_ref, qkvb_ref, gg_ref, *ubuf_refs, tm, tiles_per_seq):
    i = pl.program_id(0)
    xb = x_ref[...].astype(BF16)
    halo = MLSTM_CONV - 1

    @pl.when(i == 0)
    def _():
        for ubuf_ref in ubuf_refs:
            ubuf_ref[tm:tm + SUBLANES, :] = jnp.zeros((SUBLANES, PROJ_CHUNK), F32)

    at_start = jnp.full((SUBLANES, PROJ_CHUNK), i % tiles_per_seq, jnp.int32) == 0
    for c, ubuf_ref in enumerate(ubuf_refs):
        cs = slice(c * PROJ_CHUNK, (c + 1) * PROJ_CHUNK)
        u = _dot(xb, wqk_ref[:, cs]) + bqk_ref[:, cs]
        ubuf_ref[0:SUBLANES, :] = jnp.where(at_start, 0.0, ubuf_ref[tm:tm + SUBLANES, :])
        ubuf_ref[SUBLANES:SUBLANES + tm, :] = u
        conv = cw_ref[halo:halo + 1, cs] * u
        for k in range(halo):
            off = SUBLANES - halo + k
            conv = conv + cw_ref[k:k + 1, cs] * ubuf_ref[off:off + tm, :]
        act = conv * jax.nn.sigmoid(conv)
        if c * PROJ_CHUNK >= MLSTM_WIDTH:
            act = act * (MLSTM_HEAD_DIM ** -0.5)
        qk_ref[:, cs] = act.astype(qk_ref.dtype)

    for c in range(2 * MLSTM_WIDTH // PROJ_CHUNK):
        cs = slice(c * PROJ_CHUNK, (c + 1) * PROJ_CHUNK)
        vo_ref[:, cs] = (_dot(xb, wvo_ref[:, cs]) + bvo_ref[:, cs]).astype(vo_ref.dtype)
    if_ref[...] = _dot(xb, wif_ref[...]) + bif_ref[...]
    for c in range(3 * MOBA_WIDTH // PROJ_CHUNK):
        cs = slice(c * PROJ_CHUNK, (c + 1) * PROJ_CHUNK)
        qkvb_ref[:, cs] = (_dot(xb, wb_ref[:, cs]) + bb_ref[:, cs]).astype(qkvb_ref.dtype)
    for c in range(2 * D_MODEL // PROJ_CHUNK):
        cs = slice(c * PROJ_CHUNK, (c + 1) * PROJ_CHUNK)
        gg_ref[:, cs] = (_dot(xb, wg_ref[:, cs]) + bg_ref[:, cs]).astype(gg_ref.dtype)


def _in_proj(x2d, w_in, b_in, conv_qk, seq_len, tm=512):
    m_rows = x2d.shape[0]
    w4 = 4 * MLSTM_WIDTH
    nh = MLSTM_HEADS
    b0 = w4 + 2 * nh
    scale_b = MOBA_HEAD_DIM ** -0.5 * LOG2E
    col_scale = jnp.concatenate([jnp.full((MOBA_WIDTH,), scale_b, F32), jnp.ones((2 * MOBA_WIDTH,), F32)])
    wqk = w_in[:, :2 * MLSTM_WIDTH].astype(BF16)
    bqk = b_in[None, :2 * MLSTM_WIDTH]
    wvo = w_in[:, 2 * MLSTM_WIDTH:w4].astype(BF16)
    bvo = b_in[None, 2 * MLSTM_WIDTH:w4]
    wif = jnp.pad(w_in[:, w4:b0], ((0, 0), (0, LANES - 2 * nh))).astype(BF16)
    bif = jnp.pad(b_in[w4:b0], (0, LANES - 2 * nh))[None, :]
    wb = (w_in[:, b0:b0 + 3 * MOBA_WIDTH] * col_scale).astype(BF16)
    bb = (b_in[b0:b0 + 3 * MOBA_WIDTH] * col_scale)[None, :]
    wg = w_in[:, b0 + 3 * MOBA_WIDTH:].astype(BF16)
    bg = b_in[None, b0 + 3 * MOBA_WIDTH:]

    row = lambda n: pl.BlockSpec((tm, n), lambda i: (i, 0))
    n_chunks = 2 * MLSTM_WIDTH // PROJ_CHUNK
    return pl.pallas_call(
        functools.partial(_inproj_kernel, tm=tm, tiles_per_seq=seq_len // tm),
        grid=(m_rows // tm,),
        in_specs=[row(D_MODEL),
                  _resident(wqk.shape), _resident(bqk.shape), _resident(conv_qk.shape),
                  _resident(wvo.shape), _resident(bvo.shape),
                  _resident(wif.shape), _resident(bif.shape),
                  _resident(wb.shape), _resident(bb.shape),
                  _resident(wg.shape), _resident(bg.shape)],
        out_specs=[row(2 * MLSTM_WIDTH), row(2 * MLSTM_WIDTH), row(LANES), row(3 * MOBA_WIDTH), row(2 * D_MODEL)],
        out_shape=[jax.ShapeDtypeStruct((m_rows, 2 * MLSTM_WIDTH), BF16),
                   jax.ShapeDtypeStruct((m_rows, 2 * MLSTM_WIDTH), BF16),
                   jax.ShapeDtypeStruct((m_rows, LANES), F32),
                   jax.ShapeDtypeStruct((m_rows, 3 * MOBA_WIDTH), BF16),
                   jax.ShapeDtypeStruct((m_rows, 2 * D_MODEL), BF16)],
        scratch_shapes=[pltpu.VMEM((tm + SUBLANES, PROJ_CHUNK), F32)] * n_chunks,
        compiler_params=pltpu.CompilerParams(dimension_semantics=("arbitrary",),
                                             vmem_limit_bytes=VMEM_LIMIT_BYTES),
        name="in_proj",
    )(x2d, wqk, bqk, conv_qk, wvo, bvo, wif, bif, wb, bb, wg, bg)


def _log_sigmoid(x):
    return jnp.minimum(x, 0.0) - jnp.log1p(jnp.exp(-jnp.abs(x)))


def _mlstm_kernel(qk_ref, vo_ref, ift_ref, gain_ref, ya_ref, *state_refs):
    c = pl.program_id(1)
    L = MLSTM_CHUNK
    nh = MLSTM_HEADS
    dh = MLSTM_HEAD_DIM
    c_refs, n_refs, m_refs = state_refs[:nh], state_refs[nh:2 * nh], state_refs[2 * nh:]

    @pl.when(c == 0)
    def _():
        for ref in state_refs:
            ref[...] = jnp.zeros_like(ref)

    rows = ift_ref[0]
    lane = lax.broadcasted_iota(jnp.int32, rows.shape, 1)
    cum = _log_sigmoid(rows)
    shift = 1
    while shift < L:
        cum = cum + jnp.where(lane >= shift, pltpu.roll(cum, shift, axis=1), 0.0)
        shift *= 2

    ri = lax.broadcasted_iota(jnp.int32, (L, L), 0)
    ci = lax.broadcasted_iota(jnp.int32, (L, L), 1)
    eye = ri == ci
    causal = ri >= ci

    for h in range(nh):
        c_ref, n_ref, m_ref = c_refs[h], n_refs[h], m_refs[h]
        hs = slice(h * dh, (h + 1) * dh)
        li_row = rows[h:h + 1, :]
        b_row = cum[nh + h:nh + h + 1, :]
        b_col = jnp.sum(jnp.where(eye, b_row, 0.0), axis=1, keepdims=True)
        li_col = jnp.sum(jnp.where(eye, li_row, 0.0), axis=1, keepdims=True)

        m_prev = m_ref[...]
        d_intra = jnp.where(causal, b_col - b_row + li_row, -jnp.inf)
        inter = b_col + m_prev
        m_q = jnp.maximum(inter, jnp.max(d_intra, axis=1, keepdims=True))
        w_inter = jnp.exp(inter - m_q)

        qb = qk_ref[:, hs]
        kb = qk_ref[:, MLSTM_WIDTH + h * dh:MLSTM_WIDTH + (h + 1) * dh]
        vb = vo_ref[:, hs]
        s = _dot_nt(qb, kb) * jnp.exp(d_intra - m_q)
        num = w_inter * _dot(qb, c_ref[...].astype(BF16)) + _dot(s.astype(BF16), vb)
        qn = jnp.sum(qb.astype(F32) * n_ref[...], axis=1, keepdims=True)
        den = w_inter * qn + jnp.sum(s, axis=1, keepdims=True)
        hh = num / jnp.maximum(jnp.abs(den), jnp.exp(-m_q))

        mu = jnp.mean(hh, axis=1, keepdims=True)
        hc = hh - mu
        var = jnp.mean(hc * hc, axis=1, keepdims=True)
        hn = hc * lax.rsqrt(var + LN_EPS) * gain_ref[:, hs]
        og = vo_ref[:, MLSTM_WIDTH + h * dh:MLSTM_WIDTH + (h + 1) * dh].astype(F32)
        ya_ref[:, hs] = (jax.nn.sigmoid(og) * hn).astype(ya_ref.dtype)

        b_last = b_row[:, L - 1:L]
        d_state_row = b_last - b_row + li_row
        m_new = jnp.maximum(b_last + m_prev, jnp.max(d_state_row, axis=1, keepdims=True))
        w_prev = jnp.exp(b_last + m_prev - m_new)
        w_k = jnp.exp(b_last - b_col + li_col - m_new)
        kw = kb.astype(F32) * w_k
        c_ref[...] = w_prev * c_ref[...] + _dot(kw.T.astype(BF16), vb)
        n_ref[...] = w_prev * n_ref[...] + jnp.sum(kw, axis=0, keepdims=True)
        m_ref[...] = m_new


def _mlstm(qk, vo, ift, gain, batch, seq_len):
    L = MLSTM_CHUNK
    dh = MLSTM_HEAD_DIM
    nh = MLSTM_HEADS
    nc = seq_len // L
    m_rows = qk.shape[0]
    rows = lambda n: pl.BlockSpec((L, n), lambda b, c: (b * nc + c, 0))
    return pl.pallas_call(
        _mlstm_kernel,
        grid=(batch, nc),
        in_specs=[rows(2 * MLSTM_WIDTH), rows(2 * MLSTM_WIDTH),
                  pl.BlockSpec((1, 2 * nh, L), lambda b, c: (b, 0, c)),
                  pl.BlockSpec((1, MLSTM_WIDTH), lambda b, c: (0, 0))],
        out_specs=rows(MLSTM_WIDTH),
        out_shape=jax.ShapeDtypeStruct((m_rows, MLSTM_WIDTH), BF16),
        scratch_shapes=([pltpu.VMEM((dh, dh), F32)] * nh + [pltpu.VMEM((1, dh), F32)] * nh
                        + [pltpu.VMEM((1, 1), F32)] * nh),
        compiler_params=pltpu.CompilerParams(dimension_semantics=("arbitrary", "arbitrary"),
                                             vmem_limit_bytes=VMEM_LIMIT_BYTES),
        name="mlstm",
    )(qk, vo, ift, gain)


def _t5_bucket(rel):
    n = jnp.maximum(rel, 0)
    max_exact = NUM_BUCKETS // 2
    nf = jnp.maximum(n, max_exact).astype(F32)
    large = max_exact + (jnp.log(nf / max_exact) / math.log(REL_MAX_DISTANCE / max_exact)
                         * (NUM_BUCKETS - max_exact)).astype(jnp.int32)
    large = jnp.minimum(large, NUM_BUCKETS - 1)
    return jnp.where(n < max_exact, n, large)


def _moba_kernel(rb_ref, q_ref, k_ref, v_ref, bkt_own_ref, bkt_prev_ref, yb_ref,
                 kmean_ref, bias_ref, mask_ref, sa_ref, sb_ref, m_ref, l_ref, acc_ref, *, nb):
    h = pl.program_id(1)
    step = pl.program_id(2)
    blk = MOBA_BLOCK
    nq = MOBA_QBLOCKS * blk
    G = MOBA_GROUP
    i_lo = MOBA_QBLOCKS * step
    i_hi = i_lo + 1
    far_bias = rb_ref[NUM_BUCKETS - 1, h] * LOG2E

    @pl.when(step == 0)
    def _():
        def mean_body(j, carry):
            kj = k_ref[0, pl.ds(pl.multiple_of(j * blk, blk), blk), :].astype(F32)
            kmean_ref[pl.ds(j, 1), :] = jnp.sum(kj, axis=0, keepdims=True) * (1.0 / blk)
            return carry
        lax.fori_loop(0, nb, mean_body, 0)
        bo = bkt_own_ref[...]
        bp = bkt_prev_ref[...]
        own = jnp.full(bo.shape, NEG, F32)
        prev = jnp.zeros(bp.shape, F32)
        for bucket in range(NUM_BUCKETS):
            val = rb_ref[bucket, h] * LOG2E
            own = jnp.where(bo == bucket, val, own)
            prev = jnp.where(bp == bucket, val, prev)
        bias_ref[0, :, 0:blk] = jnp.full((blk, blk), NEG, F32)
        bias_ref[0, :, blk:nq] = own
        bias_ref[1, :, 0:blk] = own
        bias_ref[1, :, blk:nq] = prev
        bias_ref[2, :, 0:blk] = prev
        bias_ref[2, :, blk:nq] = jnp.full((blk, blk), far_bias, F32)

    qb = q_ref[...]

    km = kmean_ref[...]
    km_hi = km.astype(BF16)
    r1 = km - km_hi.astype(F32)
    km_mid = r1.astype(BF16)
    km_lo = (r1 - km_mid.astype(F32)).astype(BF16)
    gate = _dot_nt(km_hi, qb) + _dot_nt(km_mid, qb) + _dot_nt(km_lo, qb)
    row = lax.broadcasted_iota(jnp.int32, gate.shape, 0)
    lane = lax.broadcasted_iota(jnp.int32, (1, nq), 1)
    own_blk = jnp.where(lane < blk, i_lo, i_hi)
    gate = jnp.where(row < own_blk, gate, -jnp.inf)
    chosen = jnp.zeros(gate.shape, jnp.bool_)
    for r in range(MOBA_TOPK):
        mx = jnp.max(gate, axis=0, keepdims=True)
        first = jnp.min(jnp.where(gate == mx, row, nb), axis=0, keepdims=True)
        pick = row == first + jnp.where(own_blk > r, 0, 2 * nb)
        chosen = jnp.logical_or(chosen, pick)
        gate = jnp.where(pick, -jnp.inf, gate)
    mask_ref[0:MASK_PAD, :] = jnp.full((MASK_PAD, nq), NEG, F32)
    mask_ref[MASK_PAD:MASK_PAD + nb, :] = jnp.where(chosen, 0.0, NEG)

    def mask_row(j):
        return mask_ref[pl.ds(j + MASK_PAD, 1), :]

    def logits_group(t, s_ref):
        for r in range(G):
            jc = jnp.maximum(i_hi - G * t - r, 0)
            kj = k_ref[0, pl.ds(pl.multiple_of(jc * blk, blk), blk), :]
            s_ref[r * blk:(r + 1) * blk, :] = _dot_nt(kj, qb)

    def softmax_pv(t, s_ref, rows, m_old):
        cmax = [jnp.max(s_ref[r * blk:(r + 1) * blk, :], axis=0, keepdims=True) + rows[r] for r in range(G)]
        m_new = functools.reduce(jnp.maximum, cmax)
        if m_old is not None:
            m_new = jnp.maximum(m_new, m_old)
        lsum = jnp.zeros((1, nq), F32)
        pv = jnp.zeros((MOBA_HEAD_DIM, nq), F32)
        for r in range(G):
            p = jnp.exp2(s_ref[r * blk:(r + 1) * blk, :] - (m_new - rows[r]))
            lsum = lsum + jnp.sum(p, axis=0, keepdims=True)
            jc = jnp.maximum(i_hi - G * t - r, 0)
            vj = v_ref[0, pl.ds(pl.multiple_of(jc * blk, blk), blk), :]
            pv = pv + _dot_tn(vj, p.astype(BF16))
        return m_new, lsum, pv

    def far_step(t, s_ref, next_ref):
        if next_ref is not None:
            logits_group(t + 1, next_ref)
        rows = [mask_row(i_hi - G * t - r) + far_bias for r in range(G)]
        m_old = m_ref[...]
        m_new, lsum, pv = softmax_pv(t, s_ref, rows, m_old)
        alpha = jnp.exp2(m_old - m_new)
        l_ref[...] = alpha * l_ref[...] + lsum
        acc_ref[...] = alpha * acc_ref[...] + pv
        m_ref[...] = m_new

    logits_group(0, sa_ref)
    logits_group(1, sb_ref)
    for r in range(3):
        sa_ref[r * blk:(r + 1) * blk, :] = sa_ref[r * blk:(r + 1) * blk, :] + bias_ref[r]
    rows0 = [jnp.zeros((1, nq), F32),
             jnp.where(lane < blk, 0.0, mask_row(i_lo)),
             mask_row(i_lo - 1)] + [mask_row(i_hi - r) + far_bias for r in range(3, G)]
    m0, l0, pv0 = softmax_pv(0, sa_ref, rows0, None)
    m_ref[...] = m0
    l_ref[...] = l0
    acc_ref[...] = pv0

    n_far = (i_hi + G) // G - 1

    def pair_body(u, carry):
        far_step(2 * u + 1, sb_ref, sa_ref)
        far_step(2 * u + 2, sa_ref, sb_ref)
        return carry
    lax.fori_loop(0, n_far // 2, pair_body, 0)

    @pl.when(n_far % 2 == 1)
    def _():
        far_step(n_far, sb_ref, None)

    out_t = acc_ref[...] / l_ref[...]
    yb_ref[...] = out_t.T.astype(yb_ref.dtype)


def _moba(qkvb, rel_bias, batch, seq_len):
    blk = MOBA_BLOCK
    dh = MOBA_HEAD_DIM
    nh = MOBA_HEADS
    nb = seq_len // blk
    nq = MOBA_QBLOCKS * blk
    n_steps = nb // MOBA_QBLOCKS
    m_rows = qkvb.shape[0]
    qkvb3 = qkvb.reshape(batch, seq_len, 3 * MOBA_WIDTH)
    pos = jnp.arange(blk, dtype=jnp.int32)
    rel_own = pos[None, :] - pos[:, None]
    bkt_own = jnp.where(rel_own >= 0, _t5_bucket(rel_own), -1)
    bkt_prev = _t5_bucket(rel_own + blk)
    full2 = lambda shape: pl.BlockSpec(shape, lambda b, h, i: (0, 0))
    return pl.pallas_call(
        functools.partial(_moba_kernel, nb=nb),
        grid=(batch, nh, n_steps),
        in_specs=[pl.BlockSpec(memory_space=pltpu.SMEM),
                  pl.BlockSpec((nq, dh), lambda b, h, i: (b * n_steps + i, h)),
                  pl.BlockSpec((1, seq_len, dh), lambda b, h, i: (b, 0, nh + h)),
                  pl.BlockSpec((1, seq_len, dh), lambda b, h, i: (b, 0, 2 * nh + h)),
                  full2((blk, blk)), full2((blk, blk))],
        out_specs=pl.BlockSpec((nq, dh), lambda b, h, i: (b * n_steps + i, h)),
        out_shape=jax.ShapeDtypeStruct((m_rows, MOBA_WIDTH), BF16),
        scratch_shapes=[pltpu.VMEM((nb, dh), F32),
                        pltpu.VMEM((3, blk, nq), F32),
                        pltpu.VMEM((MASK_PAD + nb, nq), F32),
                        pltpu.VMEM((MOBA_GROUP * blk, nq), F32), pltpu.VMEM((MOBA_GROUP * blk, nq), F32),
                        pltpu.VMEM((1, nq), F32), pltpu.VMEM((1, nq), F32), pltpu.VMEM((dh, nq), F32)],
        compiler_params=pltpu.CompilerParams(dimension_semantics=("arbitrary", "arbitrary", "arbitrary"),
                                             vmem_limit_bytes=VMEM_LIMIT_BYTES),
        name="moba",
    )(rel_bias, qkvb, qkvb3, qkvb3, bkt_own, bkt_prev)


def _mixtail_kernel(x_ref, ya_ref, yb_ref, ga_ref, gb_ref, wa_ref, wb_ref, wo_ref, g_ref, b_ref, out_ref):
    pa = _dot(ya_ref[...], wa_ref[...])
    pb = _dot(yb_ref[...], wb_ref[...])
    merged = (jax.nn.sigmoid(ga_ref[...].astype(F32)) * pa
              + jax.nn.sigmoid(gb_ref[...].astype(F32)) * pb)
    mix = _dot(merged.astype(BF16), wo_ref[...])
    z = ALPHA * x_ref[...] + mix
    out_ref[...] = _layer_norm_rows(z, g_ref[...], b_ref[...])


def _mix_tail(x2d, ya, yb, gg, w_a, w_b, w_o, ln_g, ln_b, tm=512):
    m_rows = x2d.shape[0]
    row = lambda j: pl.BlockSpec((tm, D_MODEL), lambda i: (i, j))
    wa = w_a.astype(BF16)
    wb = w_b.astype(BF16)
    wo = w_o.astype(BF16)
    return pl.pallas_call(
        _mixtail_kernel,
        grid=(m_rows // tm,),
        in_specs=[row(0), row(0), row(0), row(0), row(1),
                  _resident(wa.shape), _resident(wb.shape), _resident(wo.shape),
                  _resident((1, D_MODEL)), _resident((1, D_MODEL))],
        out_specs=row(0),
        out_shape=jax.ShapeDtypeStruct((m_rows, D_MODEL), F32),
        compiler_params=pltpu.CompilerParams(dimension_semantics=("arbitrary",),
                                             vmem_limit_bytes=VMEM_LIMIT_BYTES),
        name="mix_tail",
    )(x2d, ya, yb, gg, gg, wa, wb, wo, ln_g[None, :], ln_b[None, :])


def _ffn_kernel(x_ref, wup_ref, cw_ref, wdn_ref, g_ref, b_ref, out_ref,
                xb_ref, ua_ref, ub_ref, carry_ref, acc_ref, *, tm, tiles_per_seq):
    i = pl.program_id(0)
    halo = FFN_CONV - 1
    xb_ref[...] = x_ref[...].astype(BF16)
    acc_ref[...] = jnp.zeros_like(acc_ref)

    @pl.when(i == 0)
    def _():
        carry_ref[...] = jnp.zeros_like(carry_ref)

    at_start = jnp.full((SUBLANES, 2 * FFN_CHUNK), i % tiles_per_seq, jnp.int32) == 0

    def up_proj(c, u_ref):
        u = _dot(xb_ref[...], wup_ref[c])
        u_ref[0:SUBLANES, :] = jnp.where(at_start, 0.0, carry_ref[c])
        u_ref[SUBLANES:SUBLANES + tm, :] = u
        carry_ref[c] = u[tm - SUBLANES:tm, :]

    def act_down(c, u_ref):
        cw = cw_ref[c]
        conv = cw[halo:halo + 1, :] * u_ref[SUBLANES:SUBLANES + tm, :]
        for k in range(halo):
            off = SUBLANES - halo + k
            conv = conv + cw[k:k + 1, :] * u_ref[off:off + tm, :]
        gate = conv[:, :FFN_CHUNK]
        up = conv[:, FFN_CHUNK:]
        act = (gate * jax.nn.sigmoid(gate) * up).astype(BF16)
        acc_ref[...] += _dot(act, wdn_ref[c])

    up_proj(0, ua_ref)

    def pair_body(p, carry):
        up_proj(2 * p + 1, ub_ref)
        act_down(2 * p, ua_ref)
        up_proj(2 * p + 2, ua_ref)
        act_down(2 * p + 1, ub_ref)
        return carry
    lax.fori_loop(0, N_FFN_CHUNKS // 2, pair_body, 0)
    act_down(N_FFN_CHUNKS - 1, ua_ref)

    z = ALPHA * x_ref[...] + acc_ref[...]
    out_ref[...] = _layer_norm_rows(z, g_ref[...], b_ref[...])


def _ffn(x2d, w_up, conv_ffn, w_down, ln_g, ln_b, seq_len, tm=512):
    m_rows = x2d.shape[0]
    nch = N_FFN_CHUNKS
    fc = FFN_CHUNK

    def pair(t):
        lead = t.shape[:-1]
        t = t.reshape(*lead, 2, nch, fc)
        t = jnp.moveaxis(t, -2, 0)
        return t.reshape(nch, *lead, 2 * fc)

    wup = pair(w_up).astype(BF16)
    cw = pair(conv_ffn)
    wdn = w_down.reshape(nch, fc, D_MODEL).astype(BF16)
    row = pl.BlockSpec((tm, D_MODEL), lambda i: (i, 0))
    return pl.pallas_call(
        functools.partial(_ffn_kernel, tm=tm, tiles_per_seq=seq_len // tm),
        grid=(m_rows // tm,),
        in_specs=[row, _resident(wup.shape), _resident(cw.shape), _resident(wdn.shape),
                  _resident((1, D_MODEL)), _resident((1, D_MODEL))],
        out_specs=row,
        out_shape=jax.ShapeDtypeStruct((m_rows, D_MODEL), F32),
        scratch_shapes=[pltpu.VMEM((tm, D_MODEL), BF16),
                        pltpu.VMEM((tm + SUBLANES, 2 * fc), F32), pltpu.VMEM((tm + SUBLANES, 2 * fc), F32),
                        pltpu.VMEM((nch, SUBLANES, 2 * fc), F32),
                        pltpu.VMEM((tm, D_MODEL), F32)],
        compiler_params=pltpu.CompilerParams(dimension_semantics=("arbitrary",),
                                             vmem_limit_bytes=VMEM_LIMIT_BYTES),
        name="ffn",
    )(x2d, wup, cw, wdn, ln_g[None, :], ln_b[None, :])


def kernel(x, w_in, b_in, conv_qk, mlstm_norm, rel_bias, w_branch_a, w_branch_b, w_out,
           ln1_g, ln1_b, w_up, conv_ffn, w_down, ln2_g, ln2_b):
    batch, seq_len, d_model = x.shape
    assert d_model == D_MODEL
    assert seq_len % (MOBA_GROUP * MOBA_BLOCK) == 0
    h = x.reshape(batch * seq_len, d_model)
    for l in range(DEPTH):
        qk, vo, ifp, qkvb, gg = _in_proj(h, w_in[l], b_in[l], conv_qk[l], seq_len)
        ift = ifp[:, :2 * MLSTM_HEADS].reshape(batch, seq_len, 2 * MLSTM_HEADS).transpose(0, 2, 1)
        ya = _mlstm(qk, vo, ift, mlstm_norm[l][None, :], batch, seq_len)
        yb = _moba(qkvb, rel_bias, batch, seq_len)
        h = _mix_tail(h, ya, yb, gg, w_branch_a[l], w_branch_b[l], w_out[l], ln1_g[l], ln1_b[l])
        h = _ffn(h, w_up[l], conv_ffn[l], w_down[l], ln2_g[l], ln2_b[l], seq_len)
    return h.reshape(batch, seq_len, d_model)
```

```python
import functools
import math

import jax
import jax.numpy as jnp
from jax import lax
from jax.experimental import pallas as pl
from jax.experimental.pallas import tpu as pltpu

D_MODEL = 1024
DEPTH = 2
MLSTM_HEADS = 4
MLSTM_HEAD_DIM = 256
MLSTM_WIDTH = MLSTM_HEADS * MLSTM_HEAD_DIM
MLSTM_CONV = 4
MOBA_HEADS = 8
MOBA_HEAD_DIM = 128
MOBA_WIDTH = MOBA_HEADS * MOBA_HEAD_DIM
MOBA_BLOCK = 256
MOBA_TOPK = 3
NUM_BUCKETS = 32
REL_MAX_DISTANCE = 128
D_FF = 2816
FFN_CONV = 3
ALPHA = (2 * DEPTH) ** 0.25
LN_EPS = 1e-5

SUBLANES = 8
LANES = 128
VMEM_LIMIT_BYTES = 56 * 1024 * 1024

MLSTM_CHUNK = 256
FFN_CHUNK = 256
N_FFN_CHUNKS = D_FF // FFN_CHUNK
PROJ_CHUNK = 512
NEG = -1e30
MOBA_GROUP = 4
MOBA_QBLOCKS = 2
LOG2E = math.log2(math.e)
MASK_PAD = SUBLANES

BF16 = jnp.bfloat16
F32 = jnp.float32


def _dot(a, b):
    return jnp.dot(a, b, preferred_element_type=F32)


def _dot_nt(a, b):
    return lax.dot_general(a, b, (((1,), (1,)), ((), ())), preferred_element_type=F32)


def _resident(shape):
    nd = len(shape)
    return pl.BlockSpec(shape, lambda *_: (0,) * nd, pipeline_mode=pl.Buffered(1))


def _layer_norm_rows(z, g, b):
    mu = jnp.mean(z, axis=-1, keepdims=True)
    zc = z - mu
    var = jnp.mean(zc * zc, axis=-1, keepdims=True)
    return zc * lax.rsqrt(var + LN_EPS) * g + b


def _inproj_kernel(x_ref, wqk_ref, bqk_ref, cw_ref, wvo_ref, bvo_ref, wif_ref, bif_ref,
                   wb_ref, bb_ref, wg_ref, bg_ref,
                   qk_ref, vo_ref, if_ref, qkvb_ref, gg_ref, *ubuf_refs, tm, tiles_per_seq):
    i = pl.program_id(0)
    xb = x_ref[...].astype(BF16)
    halo = MLSTM_CONV - 1

    @pl.when(i == 0)
    def _():
        for ubuf_ref in ubuf_refs:
            ubuf_ref[tm:tm + SUBLANES, :] = jnp.zeros((SUBLANES, PROJ_CHUNK), F32)

    at_start = jnp.full((SUBLANES, PROJ_CHUNK), i % tiles_per_seq, jnp.int32) == 0
    for c, ubuf_ref in enumerate(ubuf_refs):
        cs = slice(c * PROJ_CHUNK, (c + 1) * PROJ_CHUNK)
        u = _dot(xb, wqk_ref[:, cs]) + bqk_ref[:, cs]
        ubuf_ref[0:SUBLANES, :] = jnp.where(at_start, 0.0, ubuf_ref[tm:tm + SUBLANES, :])
        ubuf_ref[SUBLANES:SUBLANES + tm, :] = u
        conv = cw_ref[halo:halo + 1, cs] * u
        for k in range(halo):
            off = SUBLANES - halo + k
            conv = conv + cw_ref[k:k + 1, cs] * ubuf_ref[off:off + tm, :]
        act = conv * jax.nn.sigmoid(conv)
        if c * PROJ_CHUNK >= MLSTM_WIDTH:
            act = act * (MLSTM_HEAD_DIM ** -0.5)
        qk_ref[:, cs] = act.astype(qk_ref.dtype)

    for c in range(2 * MLSTM_WIDTH // PROJ_CHUNK):
        cs = slice(c * PROJ_CHUNK, (c + 1) * PROJ_CHUNK)
        vo_ref[:, cs] = (_dot(xb, wvo_ref[:, cs]) + bvo_ref[:, cs]).astype(vo_ref.dtype)
    if_ref[...] = _dot(xb, wif_ref[...]) + bif_ref[...]
    for c in range(3 * MOBA_WIDTH // PROJ_CHUNK):
        cs = slice(c * PROJ_CHUNK, (c + 1) * PROJ_CHUNK)
        qkvb_ref[:, cs] = (_dot(xb, wb_ref[:, cs]) + bb_ref[:, cs]).astype(qkvb_ref.dtype)
    for c in range(2 * D_MODEL // PROJ_CHUNK):
        cs = slice(c * PROJ_CHUNK, (c + 1) * PROJ_CHUNK)
        gg_ref[:, cs] = (_dot(xb, wg_ref[:, cs]) + bg_ref[:, cs]).astype(gg_ref.dtype)


def _in_proj(x2d, w_in, b_in, conv_qk, seq_len, tm=512):
    m_rows = x2d.shape[0]
    w4 = 4 * MLSTM_WIDTH
    nh = MLSTM_HEADS
    b0 = w4 + 2 * nh
    scale_b = MOBA_HEAD_DIM ** -0.5 * LOG2E
    col_scale = jnp.concatenate([jnp.full((MOBA_WIDTH,), scale_b, F32), jnp.ones((2 * MOBA_WIDTH,), F32)])
    wqk = w_in[:, :2 * MLSTM_WIDTH].astype(BF16)
    bqk = b_in[None, :2 * MLSTM_WIDTH]
    wvo = w_in[:, 2 * MLSTM_WIDTH:w4].astype(BF16)
    bvo = b_in[None, 2 * MLSTM_WIDTH:w4]
    wif = jnp.pad(w_in[:, w4:b0], ((0, 0), (0, LANES - 2 * nh))).astype(BF16)
    bif = jnp.pad(b_in[w4:b0], (0, LANES - 2 * nh))[None, :]
    wb = (w_in[:, b0:b0 + 3 * MOBA_WIDTH] * col_scale).astype(BF16)
    bb = (b_in[b0:b0 + 3 * MOBA_WIDTH] * col_scale)[None, :]
    wg = w_in[:, b0 + 3 * MOBA_WIDTH:].astype(BF16)
    bg = b_in[None, b0 + 3 * MOBA_WIDTH:]

    row = lambda n: pl.BlockSpec((tm, n), lambda i: (i, 0))
    n_chunks = 2 * MLSTM_WIDTH // PROJ_CHUNK
    return pl.pallas_call(
        functools.partial(_inproj_kernel, tm=tm, tiles_per_seq=seq_len // tm),
        grid=(m_rows // tm,),
        in_specs=[row(D_MODEL),
                  _resident(wqk.shape), _resident(bqk.shape), _resident(conv_qk.shape),
                  _resident(wvo.shape), _resident(bvo.shape),
                  _resident(wif.shape), _resident(bif.shape),
                  _resident(wb.shape), _resident(bb.shape),
                  _resident(wg.shape), _resident(bg.shape)],
        out_specs=[row(2 * MLSTM_WIDTH), row(2 * MLSTM_WIDTH), row(LANES), row(3 * MOBA_WIDTH), row(2 * D_MODEL)],
        out_shape=[jax.ShapeDtypeStruct((m_rows, 2 * MLSTM_WIDTH), BF16),
                   jax.ShapeDtypeStruct((m_rows, 2 * MLSTM_WIDTH), BF16),
                   jax.ShapeDtypeStruct((m_rows, LANES), F32),
                   jax.ShapeDtypeStruct((m_rows, 3 * MOBA_WIDTH), BF16),
                   jax.ShapeDtypeStruct((m_rows, 2 * D_MODEL), BF16)],
        scratch_shapes=[pltpu.VMEM((tm + SUBLANES, PROJ_CHUNK), F32)] * n_chunks,
        compiler_params=pltpu.CompilerParams(dimension_semantics=("arbitrary",),
                                             vmem_limit_bytes=VMEM_LIMIT_BYTES),
        name="in_proj",
    )(x2d, wqk, bqk, conv_qk, wvo, bvo, wif, bif, wb, bb, wg, bg)


def _log_sigmoid(x):
    return jnp.minimum(x, 0.0) - jnp.log1p(jnp.exp(-jnp.abs(x)))


def _mlstm_kernel(qk_ref, vo_ref, ift_ref, gain_ref, ya_ref, *state_refs):
    c = pl.program_id(1)
    L = MLSTM_CHUNK
    nh = MLSTM_HEADS
    dh = MLSTM_HEAD_DIM
    c_refs, n_refs, m_refs = state_refs[:nh], state_refs[nh:2 * nh], state_refs[2 * nh:]

    @pl.when(c == 0)
    def _():
        for ref in state_refs:
            ref[...] = jnp.zeros_like(ref)

    rows = ift_ref[0]
    lane = lax.broadcasted_iota(jnp.int32, rows.shape, 1)
    cum = _log_sigmoid(rows)
    shift = 1
    while shift < L:
        cum = cum + jnp.where(lane >= shift, pltpu.roll(cum, shift, axis=1), 0.0)
        shift *= 2

    ri = lax.broadcasted_iota(jnp.int32, (L, L), 0)
    ci = lax.broadcasted_iota(jnp.int32, (L, L), 1)
    eye = ri == ci
    causal = ri >= ci

    for h in range(nh):
        c_ref, n_ref, m_ref = c_refs[h], n_refs[h], m_refs[h]
        hs = slice(h * dh, (h + 1) * dh)
        li_row = rows[h:h + 1, :]
        b_row = cum[nh + h:nh + h + 1, :]
        b_col = jnp.sum(jnp.where(eye, b_row, 0.0), axis=1, keepdims=True)
        li_col = jnp.sum(jnp.where(eye, li_row, 0.0), axis=1, keepdims=True)

        m_prev = m_ref[...]
        d_intra = jnp.where(causal, b_col - b_row + li_row, -jnp.inf)
        inter = b_col + m_prev
        m_q = jnp.maximum(inter, jnp.max(d_intra, axis=1, keepdims=True))
        w_inter = jnp.exp(inter - m_q)

        qb = qk_ref[:, hs]
        kb = qk_ref[:, MLSTM_WIDTH + h * dh:MLSTM_WIDTH + (h + 1) * dh]
        vb = vo_ref[:, hs]
        s = _dot_nt(qb, kb) * jnp.exp(d_intra - m_q)
        num = w_inter * _dot(qb, c_ref[...].astype(BF16)) + _dot(s.astype(BF16), vb)
        qn = jnp.sum(qb.astype(F32) * n_ref[...], axis=1, keepdims=True)
        den = w_inter * qn + jnp.sum(s, axis=1, keepdims=True)
        hh = num / jnp.maximum(jnp.abs(den), jnp.exp(-m_q))

        mu = jnp.mean(hh, axis=1, keepdims=True)
        hc = hh - mu
        var = jnp.mean(hc * hc, axis=1, keepdims=True)
        hn = hc * lax.rsqrt(var + LN_EPS) * gain_ref[:, hs]
        og = vo_ref[:, MLSTM_WIDTH + h * dh:MLSTM_WIDTH + (h + 1) * dh].astype(F32)
        ya_ref[:, hs] = (jax.nn.sigmoid(og) * hn).astype(ya_ref.dtype)

        b_last = b_row[:, L - 1:L]
        d_state_row = b_last - b_row + li_row
        m_new = jnp.maximum(b_last + m_prev, jnp.max(d_state_row, axis=1, keepdims=True))
        w_prev = jnp.exp(b_last + m_prev - m_new)
        w_k = jnp.exp(b_last - b_col + li_col - m_new)
        kw = kb.astype(F32) * w_k
        c_ref[...] = w_prev * c_ref[...] + _dot(kw.T.astype(BF16), vb)
        n_ref[...] = w_prev * n_ref[...] + jnp.sum(kw, axis=0, keepdims=True)
        m_ref[...] = m_new


def _mlstm(qk, vo, ift, gain, batch, seq_len):
    L = MLSTM_CHUNK
    dh = MLSTM_HEAD_DIM
    nh = MLSTM_HEADS
    nc = seq_len // L
    m_rows = qk.shape[0]
    rows = lambda n: pl.BlockSpec((L, n), lambda b, c: (b * nc + c, 0))
    return pl.pallas_call(
        _mlstm_kernel,
        grid=(batch, nc),
        in_specs=[rows(2 * MLSTM_WIDTH), rows(2 * MLSTM_WIDTH),
                  pl.BlockSpec((1, 2 * nh, L), lambda b, c: (b, 0, c)),
                  pl.BlockSpec((1, MLSTM_WIDTH), lambda b, c: (0, 0))],
        out_specs=rows(MLSTM_WIDTH),
        out_shape=jax.ShapeDtypeStruct((m_rows, MLSTM_WIDTH), BF16),
        scratch_shapes=([pltpu.VMEM((dh, dh), F32)] * nh + [pltpu.VMEM((1, dh), F32)] * nh
                        + [pltpu.VMEM((1, 1), F32)] * nh),
        compiler_params=pltpu.CompilerParams(dimension_semantics=("arbitrary", "arbitrary"),
                                             vmem_limit_bytes=VMEM_LIMIT_BYTES),
        name="mlstm",
    )(qk, vo, ift, gain)


def _t5_bucket(rel):
    n = jnp.maximum(rel, 0)
    max_exact = NUM_BUCKETS // 2
    nf = jnp.maximum(n, max_exact).astype(F32)
    large = max_exact + (jnp.log(nf / max_exact) / math.log(REL_MAX_DISTANCE / max_exact)
                         * (NUM_BUCKETS - max_exact)).astype(jnp.int32)
    large = jnp.minimum(large, NUM_BUCKETS - 1)
    return jnp.where(n < max_exact, n, large)


def _moba_kernel(rb_ref, q_ref, k_ref, v_ref, bkt_own_ref, bkt_prev_ref, yb_ref,
                 kmean_ref, vt_ref, bias_ref, mask_ref, sa_ref, sb_ref, pa_ref, pb_ref,
                 m_ref, l_ref, alpha_ref, acc_ref, *, nb):
    h = pl.program_id(1)
    blk = MOBA_BLOCK
    nq = MOBA_QBLOCKS * blk
    G = MOBA_GROUP
    n_steps = nb // MOBA_QBLOCKS
    far_bias = rb_ref[NUM_BUCKETS - 1, h] * LOG2E

    def prep_body(j, carry):
        rows = pl.ds(pl.multiple_of(j * blk, blk), blk)
        kmean_ref[pl.ds(j, 1), :] = jnp.sum(k_ref[rows, :].astype(F32), axis=0, keepdims=True) * (1.0 / blk)
        vt_ref[j] = v_ref[rows, :].astype(F32).T.astype(BF16)
        return carry
    lax.fori_loop(0, nb, prep_body, 0)

    bo = bkt_own_ref[...]
    bp = bkt_prev_ref[...]
    own = jnp.full(bo.shape, NEG, F32)
    prev = jnp.zeros(bp.shape, F32)
    for bucket in range(NUM_BUCKETS):
        val = rb_ref[bucket, h] * LOG2E
        own = jnp.where(bo == bucket, val, own)
        prev = jnp.where(bp == bucket, val, prev)
    bias_ref[0, :, 0:blk] = jnp.full((blk, blk), NEG, F32)
    bias_ref[0, :, blk:nq] = own
    bias_ref[1, :, 0:blk] = own
    bias_ref[1, :, blk:nq] = prev
    bias_ref[2, :, 0:blk] = prev
    bias_ref[2, :, blk:nq] = jnp.full((blk, blk), far_bias, F32)
    mask_ref[0:MASK_PAD, :] = jnp.full((MASK_PAD, nq), NEG, F32)

    lane = lax.broadcasted_iota(jnp.int32, (1, nq), 1)

    def queries(step):
        return q_ref[pl.ds(pl.multiple_of(step * nq, nq), nq), :]

    def mask_row(j):
        return mask_ref[pl.ds(j + MASK_PAD, 1), :]

    def logits_group(step, t, s_ref):
        i_hi = MOBA_QBLOCKS * step + 1
        qb = queries(step)
        for r in range(G):
            jc = jnp.maximum(i_hi - G * t - r, 0)
            kj = k_ref[pl.ds(pl.multiple_of(jc * blk, blk), blk), :]
            s_ref[r * blk:(r + 1) * blk, :] = _dot_nt(kj, qb)

    def choose_and_start(step):
        i_lo = MOBA_QBLOCKS * step
        qb = queries(step)
        km = kmean_ref[...]
        km_hi = km.astype(BF16)
        r1 = km - km_hi.astype(F32)
        km_mid = r1.astype(BF16)
        km_lo = (r1 - km_mid.astype(F32)).astype(BF16)
        gate = _dot_nt(km_hi, qb) + _dot_nt(km_mid, qb) + _dot_nt(km_lo, qb)
        row = lax.broadcasted_iota(jnp.int32, gate.shape, 0)
        own_blk = jnp.where(lane < blk, i_lo, i_lo + 1)
        gate = jnp.where(row < own_blk, gate, -jnp.inf)
        chosen = jnp.zeros(gate.shape, jnp.bool_)
        for r in range(MOBA_TOPK):
            mx = jnp.max(gate, axis=0, keepdims=True)
            first = jnp.min(jnp.where(gate == mx, row, nb), axis=0, keepdims=True)
            pick = row == first + jnp.where(own_blk > r, 0, 2 * nb)
            chosen = jnp.logical_or(chosen, pick)
            gate = jnp.where(pick, -jnp.inf, gate)
        mask_ref[MASK_PAD:MASK_PAD + nb, :] = jnp.where(chosen, 0.0, NEG)
        logits_group(step, 0, sa_ref)

    def softmax_group(s_ref, p_ref, rows, m_old):
        def slabs(x):
            return x.reshape(blk // SUBLANES, SUBLANES, nq)

        cmax = [jnp.max(slabs(s_ref[r * blk:(r + 1) * blk, :]), axis=0) + rows[r] for r in range(G)]
        m_new = jnp.max(functools.reduce(jnp.maximum, cmax), axis=0, keepdims=True)
        if m_old is not None:
            m_new = jnp.maximum(m_new, m_old)
        lsum = jnp.zeros((SUBLANES, nq), F32)
        for r in range(G):
            p = jnp.exp2(s_ref[r * blk:(r + 1) * blk, :] - (m_new - rows[r]))
            lsum = lsum + jnp.sum(slabs(p), axis=0)
            p_ref[r * blk:(r + 1) * blk, :] = p.astype(BF16)
        return m_new, jnp.sum(lsum, axis=0, keepdims=True)

    def accumulate_pv(i_hi, t, p_ref):
        vt = jnp.concatenate([vt_ref[jnp.maximum(i_hi - G * t - r, 0)] for r in range(G)], axis=1)
        acc_ref[...] = alpha_ref[...] * acc_ref[...] + _dot(vt, p_ref[...])

    def far_step(step, t, s_ref, p_ref, next_s_ref, prev_p_ref):
        i_hi = MOBA_QBLOCKS * step + 1
        if next_s_ref is not None:
            logits_group(step, t + 1, next_s_ref)
        accumulate_pv(i_hi, t - 1, prev_p_ref)
        rows = [mask_row(i_hi - G * t - r) + far_bias for r in range(G)]
        m_old = m_ref[...]
        m_new, lsum = softmax_group(s_ref, p_ref, rows, m_old)
        alpha = jnp.exp2(m_old - m_new)
        l_ref[...] = alpha * l_ref[...] + lsum
        alpha_ref[...] = alpha
        m_ref[...] = m_new

    def step_body(step, carry):
        i_lo = MOBA_QBLOCKS * step
        i_hi = i_lo + 1
        logits_group(step, 1, sb_ref)
        for r in range(3):
            sa_ref[r * blk:(r + 1) * blk, :] = sa_ref[r * blk:(r + 1) * blk, :] + bias_ref[r]
        rows0 = [jnp.zeros((1, nq), F32),
                 jnp.where(lane < blk, 0.0, mask_row(i_lo)),
                 mask_row(i_lo - 1)] + [mask_row(i_hi - r) + far_bias for r in range(3, G)]
        m0, l0 = softmax_group(sa_ref, pa_ref, rows0, None)
        m_ref[...] = m0
        l_ref[...] = l0
        acc_ref[...] = jnp.zeros_like(acc_ref)
        alpha_ref[...] = jnp.ones_like(alpha_ref)

        n_far = (i_hi + G) // G - 1

        def pair_body(u, c):
            far_step(step, 2 * u + 1, sb_ref, pb_ref, sa_ref, pa_ref)
            far_step(step, 2 * u + 2, sa_ref, pa_ref, sb_ref, pb_ref)
            return c
        lax.fori_loop(0, n_far // 2, pair_body, 0)

        @pl.when(n_far % 2 == 1)
        def _():
            far_step(step, n_far, sb_ref, pb_ref, None, pa_ref)
            accumulate_pv(i_hi, n_far, pb_ref)

        @pl.when(n_far % 2 == 0)
        def _():
            accumulate_pv(i_hi, n_far, pa_ref)

        choose_and_start(jnp.minimum(step + 1, n_steps - 1))
        out_t = acc_ref[...] / l_ref[...]
        yb_ref[pl.ds(pl.multiple_of(step * nq, nq), nq), :] = out_t.T.astype(yb_ref.dtype)
        return carry

    choose_and_start(0)
    lax.fori_loop(0, n_steps, step_body, 0)


def _moba(qkvb, rel_bias, batch, seq_len):
    blk = MOBA_BLOCK
    dh = MOBA_HEAD_DIM
    nh = MOBA_HEADS
    nb = seq_len // blk
    nq = MOBA_QBLOCKS * blk
    m_rows = qkvb.shape[0]
    pos = jnp.arange(blk, dtype=jnp.int32)
    rel_own = pos[None, :] - pos[:, None]
    bkt_own = jnp.where(rel_own >= 0, _t5_bucket(rel_own), -1)
    bkt_prev = _t5_bucket(rel_own + blk)
    full2 = lambda shape: pl.BlockSpec(shape, lambda b, h: (0, 0))
    seq = lambda off: pl.BlockSpec((seq_len, dh), lambda b, h: (b, off + h))
    return pl.pallas_call(
        functools.partial(_moba_kernel, nb=nb),
        grid=(batch, nh),
        in_specs=[pl.BlockSpec(memory_space=pltpu.SMEM),
                  seq(0), seq(nh), seq(2 * nh),
                  full2((blk, blk)), full2((blk, blk))],
        out_specs=seq(0),
        out_shape=jax.ShapeDtypeStruct((m_rows, MOBA_WIDTH), BF16),
        scratch_shapes=[pltpu.VMEM((nb, dh), F32),
                        pltpu.VMEM((nb, dh, blk), BF16),
                        pltpu.VMEM((3, blk, nq), F32),
                        pltpu.VMEM((MASK_PAD + nb, nq), F32),
                        pltpu.VMEM((MOBA_GROUP * blk, nq), F32), pltpu.VMEM((MOBA_GROUP * blk, nq), F32),
                        pltpu.VMEM((MOBA_GROUP * blk, nq), BF16), pltpu.VMEM((MOBA_GROUP * blk, nq), BF16),
                        pltpu.VMEM((1, nq), F32), pltpu.VMEM((1, nq), F32), pltpu.VMEM((1, nq), F32),
                        pltpu.VMEM((dh, nq), F32)],
        compiler_params=pltpu.CompilerParams(dimension_semantics=("arbitrary", "arbitrary"),
                                             vmem_limit_bytes=VMEM_LIMIT_BYTES),
        name="moba",
    )(rel_bias, qkvb, qkvb, qkvb, bkt_own, bkt_prev)


def _mixtail_kernel(x_ref, ya_ref, yb_ref, ga_ref, gb_ref, wa_ref, wb_ref, wo_ref, g_ref, b_ref, out_ref):
    pa = _dot(ya_ref[...], wa_ref[...])
    pb = _dot(yb_ref[...], wb_ref[...])
    merged = (jax.nn.sigmoid(ga_ref[...].astype(F32)) * pa
              + jax.nn.sigmoid(gb_ref[...].astype(F32)) * pb)
    mix = _dot(merged.astype(BF16), wo_ref[...])
    z = ALPHA * x_ref[...] + mix
    out_ref[...] = _layer_norm_rows(z, g_ref[...], b_ref[...])


def _mix_tail(x2d, ya, yb, gg, w_a, w_b, w_o, ln_g, ln_b, tm=512):
    m_rows = x2d.shape[0]
    row = lambda j: pl.BlockSpec((tm, D_MODEL), lambda i: (i, j))
    wa = w_a.astype(BF16)
    wb = w_b.astype(BF16)
    wo = w_o.astype(BF16)
    return pl.pallas_call(
        _mixtail_kernel,
        grid=(m_rows // tm,),
        in_specs=[row(0), row(0), row(0), row(0), row(1),
                  _resident(wa.shape), _resident(wb.shape), _resident(wo.shape),
                  _resident((1, D_MODEL)), _resident((1, D_MODEL))],
        out_specs=row(0),
        out_shape=jax.ShapeDtypeStruct((m_rows, D_MODEL), F32),
        compiler_params=pltpu.CompilerParams(dimension_semantics=("arbitrary",),
                                             vmem_limit_bytes=VMEM_LIMIT_BYTES),
        name="mix_tail",
    )(x2d, ya, yb, gg, gg, wa, wb, wo, ln_g[None, :], ln_b[None, :])


def _ffn_kernel(x_ref, wup_ref, cw_ref, wdn_ref, g_ref, b_ref, out_ref,
                xb_ref, ua_ref, ub_ref, carry_ref, acc_ref, *, tm, tiles_per_seq):
    i = pl.program_id(0)
    halo = FFN_CONV - 1
    xb_ref[...] = x_ref[...].astype(BF16)
    acc_ref[...] = jnp.zeros_like(acc_ref)

    @pl.when(i == 0)
    def _():
        carry_ref[...] = jnp.zeros_like(carry_ref)

    at_start = jnp.full((SUBLANES, 2 * FFN_CHUNK), i % tiles_per_seq, jnp.int32) == 0

    def up_proj(c, u_ref):
        u = _dot(xb_ref[...], wup_ref[c])
        u_ref[0:SUBLANES, :] = jnp.where(at_start, 0.0, carry_ref[c])
        u_ref[SUBLANES:SUBLANES + tm, :] = u
        carry_ref[c] = u[tm - SUBLANES:tm, :]

    def act_down(c, u_ref):
        cw = cw_ref[c]
        conv = cw[halo:halo + 1, :] * u_ref[SUBLANES:SUBLANES + tm, :]
        for k in range(halo):
            off = SUBLANES - halo + k
            conv = conv + cw[k:k + 1, :] * u_ref[off:off + tm, :]
        gate = conv[:, :FFN_CHUNK]
        up = conv[:, FFN_CHUNK:]
        act = (gate * jax.nn.sigmoid(gate) * up).astype(BF16)
        acc_ref[...] += _dot(act, wdn_ref[c])

    up_proj(0, ua_ref)

    def pair_body(p, carry):
        up_proj(2 * p + 1, ub_ref)
        act_down(2 * p, ua_ref)
        up_proj(2 * p + 2, ua_ref)
        act_down(2 * p + 1, ub_ref)
        return carry
    lax.fori_loop(0, N_FFN_CHUNKS // 2, pair_body, 0)
    act_down(N_FFN_CHUNKS - 1, ua_ref)

    z = ALPHA * x_ref[...] + acc_ref[...]
    out_ref[...] = _layer_norm_rows(z, g_ref[...], b_ref[...])


def _ffn(x2d, w_up, conv_ffn, w_down, ln_g, ln_b, seq_len, tm=512):
    m_rows = x2d.shape[0]
    nch = N_FFN_CHUNKS
    fc = FFN_CHUNK

    def pair(t):
        lead = t.shape[:-1]
        t = t.reshape(*lead, 2, nch, fc)
        t = jnp.moveaxis(t, -2, 0)
        return t.reshape(nch, *lead, 2 * fc)

    wup = pair(w_up).astype(BF16)
    cw = pair(conv_ffn)
    wdn = w_down.reshape(nch, fc, D_MODEL).astype(BF16)
    row = pl.BlockSpec((tm, D_MODEL), lambda i: (i, 0))
    return pl.pallas_call(
        functools.partial(_ffn_kernel, tm=tm, tiles_per_seq=seq_len // tm),
        grid=(m_rows // tm,),
        in_specs=[row, _resident(wup.shape), _resident(cw.shape), _resident(wdn.shape),
                  _resident((1, D_MODEL)), _resident((1, D_MODEL))],
        out_specs=row,
        out_shape=jax.ShapeDtypeStruct((m_rows, D_MODEL), F32),
        scratch_shapes=[pltpu.VMEM((tm, D_MODEL), BF16),
                        pltpu.VMEM((tm + SUBLANES, 2 * fc), F32), pltpu.VMEM((tm + SUBLANES, 2 * fc), F32),
                        pltpu.VMEM((nch, SUBLANES, 2 * fc), F32),
                        pltpu.VMEM((tm, D_MODEL), F32)],
        compiler_params=pltpu.CompilerParams(dimension_semantics=("arbitrary",),
                                             vmem_limit_bytes=VMEM_LIMIT_BYTES),
        name="ffn",
    )(x2d, wup, cw, wdn, ln_g[None, :], ln_b[None, :])


def kernel(x, w_in, b_in, conv_qk, mlstm_norm, rel_bias, w_branch_a, w_branch_b, w_out,
           ln1_g, ln1_b, w_up, conv_ffn, w_down, ln2_g, ln2_b):
    batch, seq_len, d_model = x.shape
    assert d_model == D_MODEL
    assert seq_len % (MOBA_GROUP * MOBA_BLOCK) == 0
    h = x.reshape(batch * seq_len, d_model)
    for l in range(DEPTH):
        qk, vo, ifp, qkvb, gg = _in_proj(h, w_in[l], b_in[l], conv_qk[l], seq_len)
        ift = ifp[:, :2 * MLSTM_HEADS].reshape(batch, seq_len, 2 * MLSTM_HEADS).transpose(0, 2, 1)
        ya = _mlstm(qk, vo, ift, mlstm_norm[l][None, :], batch, seq_len)
        yb = _moba(qkvb, rel_bias, batch, seq_len)
        h = _mix_tail(h, ya, yb, gg, w_branch_a[l], w_branch_b[l], w_out[l], ln1_g[l], ln1_b[l])
        h = _ffn(h, w_up[l], conv_ffn[l], w_down[l], ln2_g[l], ln2_b[l], seq_len)
    return h.reshape(batch, seq_len, d_model)
```

```python
import functools
import math

import jax
import jax.numpy as jnp
from jax import lax
from jax.experimental import pallas as pl
from jax.experimental.pallas import tpu as pltpu

D_MODEL = 1024
DEPTH = 2
MLSTM_HEADS = 4
MLSTM_HEAD_DIM = 256
MLSTM_WIDTH = MLSTM_HEADS * MLSTM_HEAD_DIM
MLSTM_CONV = 4
MOBA_HEADS = 8
MOBA_HEAD_DIM = 128
MOBA_WIDTH = MOBA_HEADS * MOBA_HEAD_DIM
MOBA_BLOCK = 256
MOBA_TOPK = 3
NUM_BUCKETS = 32
REL_MAX_DISTANCE = 128
D_FF = 2816
FFN_CONV = 3
ALPHA = (2 * DEPTH) ** 0.25
LN_EPS = 1e-5

SUBLANES = 8
LANES = 128
VMEM_LIMIT_BYTES = 56 * 1024 * 1024

MLSTM_CHUNK = 256
FFN_CHUNK = 256
N_FFN_CHUNKS = D_FF // FFN_CHUNK
PROJ_CHUNK = 512
NEG = -1e30
MOBA_GROUP = 4
MOBA_QBLOCKS = 2
MOBA_HEADS_PER_STEP = 2
LOG2E = math.log2(math.e)
MASK_PAD = SUBLANES

BF16 = jnp.bfloat16
F32 = jnp.float32


def _dot(a, b):
    return jnp.dot(a, b, preferred_element_type=F32)


def _dot_nt(a, b):
    return lax.dot_general(a, b, (((1,), (1,)), ((), ())), preferred_element_type=F32)


def _dot_tn(a, b):
    return lax.dot_general(a, b, (((0,), (0,)), ((), ())), preferred_element_type=F32)


def _resident(shape):
    nd = len(shape)
    return pl.BlockSpec(shape, lambda *_: (0,) * nd, pipeline_mode=pl.Buffered(1))


def _layer_norm_rows(z, g, b):
    mu = jnp.mean(z, axis=-1, keepdims=True)
    zc = z - mu
    var = jnp.mean(zc * zc, axis=-1, keepdims=True)
    return zc * lax.rsqrt(var + LN_EPS) * g + b


def _inproj_kernel(x_ref, wqk_ref, bqk_ref, cw_ref, wvo_ref, bvo_ref, wif_ref, bif_ref,
                   wb_ref, bb_ref, wg_ref, bg_ref,
                   qk_ref, vo_ref, if_ref, qkvb_ref, gg_ref, *ubuf_refs, tm, tiles_per_seq):
    i = pl.program_id(0)
    xb = x_ref[...].astype(BF16)
    halo = MLSTM_CONV - 1

    @pl.when(i == 0)
    def _():
        for ubuf_ref in ubuf_refs:
            ubuf_ref[tm:tm + SUBLANES, :] = jnp.zeros((SUBLANES, PROJ_CHUNK), F32)

    at_start = jnp.full((SUBLANES, PROJ_CHUNK), i % tiles_per_seq, jnp.int32) == 0
    for c, ubuf_ref in enumerate(ubuf_refs):
        cs = slice(c * PROJ_CHUNK, (c + 1) * PROJ_CHUNK)
        u = _dot(xb, wqk_ref[:, cs]) + bqk_ref[:, cs]
        ubuf_ref[0:SUBLANES, :] = jnp.where(at_start, 0.0, ubuf_ref[tm:tm + SUBLANES, :])
        ubuf_ref[SUBLANES:SUBLANES + tm, :] = u
        conv = cw_ref[halo:halo + 1, cs] * u
        for k in range(halo):
            off = SUBLANES - halo + k
            conv = conv + cw_ref[k:k + 1, cs] * ubuf_ref[off:off + tm, :]
        act = conv * jax.nn.sigmoid(conv)
        if c * PROJ_CHUNK >= MLSTM_WIDTH:
            act = act * (MLSTM_HEAD_DIM ** -0.5)
        qk_ref[:, cs] = act.astype(qk_ref.dtype)

    for c in range(2 * MLSTM_WIDTH // PROJ_CHUNK):
        cs = slice(c * PROJ_CHUNK, (c + 1) * PROJ_CHUNK)
        vo_ref[:, cs] = (_dot(xb, wvo_ref[:, cs]) + bvo_ref[:, cs]).astype(vo_ref.dtype)
    if_ref[...] = _dot(xb, wif_ref[...]) + bif_ref[...]
    for c in range(3 * MOBA_WIDTH // PROJ_CHUNK):
        cs = slice(c * PROJ_CHUNK, (c + 1) * PROJ_CHUNK)
        qkvb_ref[:, cs] = (_dot(xb, wb_ref[:, cs]) + bb_ref[:, cs]).astype(qkvb_ref.dtype)
    for c in range(2 * D_MODEL // PROJ_CHUNK):
        cs = slice(c * PROJ_CHUNK, (c + 1) * PROJ_CHUNK)
        gg_ref[:, cs] = (_dot(xb, wg_ref[:, cs]) + bg_ref[:, cs]).astype(gg_ref.dtype)


def _in_proj(x2d, w_in, b_in, conv_qk, seq_len, tm=512):
    m_rows = x2d.shape[0]
    w4 = 4 * MLSTM_WIDTH
    nh = MLSTM_HEADS
    b0 = w4 + 2 * nh
    scale_b = MOBA_HEAD_DIM ** -0.5 * LOG2E
    col_scale = jnp.concatenate([jnp.full((MOBA_WIDTH,), scale_b, F32), jnp.ones((2 * MOBA_WIDTH,), F32)])
    wqk = w_in[:, :2 * MLSTM_WIDTH].astype(BF16)
    bqk = b_in[None, :2 * MLSTM_WIDTH]
    wvo = w_in[:, 2 * MLSTM_WIDTH:w4].astype(BF16)
    bvo = b_in[None, 2 * MLSTM_WIDTH:w4]
    wif = jnp.pad(w_in[:, w4:b0], ((0, 0), (0, LANES - 2 * nh))).astype(BF16)
    bif = jnp.pad(b_in[w4:b0], (0, LANES - 2 * nh))[None, :]
    wb = (w_in[:, b0:b0 + 3 * MOBA_WIDTH] * col_scale).astype(BF16)
    bb = (b_in[b0:b0 + 3 * MOBA_WIDTH] * col_scale)[None, :]
    wg = w_in[:, b0 + 3 * MOBA_WIDTH:].astype(BF16)
    bg = b_in[None, b0 + 3 * MOBA_WIDTH:]

    row = lambda n: pl.BlockSpec((tm, n), lambda i: (i, 0))
    n_chunks = 2 * MLSTM_WIDTH // PROJ_CHUNK
    return pl.pallas_call(
        functools.partial(_inproj_kernel, tm=tm, tiles_per_seq=seq_len // tm),
        grid=(m_rows // tm,),
        in_specs=[row(D_MODEL),
                  _resident(wqk.shape), _resident(bqk.shape), _resident(conv_qk.shape),
                  _resident(wvo.shape), _resident(bvo.shape),
                  _resident(wif.shape), _resident(bif.shape),
                  _resident(wb.shape), _resident(bb.shape),
                  _resident(wg.shape), _resident(bg.shape)],
        out_specs=[row(2 * MLSTM_WIDTH), row(2 * MLSTM_WIDTH), row(LANES), row(3 * MOBA_WIDTH), row(2 * D_MODEL)],
        out_shape=[jax.ShapeDtypeStruct((m_rows, 2 * MLSTM_WIDTH), BF16),
                   jax.ShapeDtypeStruct((m_rows, 2 * MLSTM_WIDTH), BF16),
                   jax.ShapeDtypeStruct((m_rows, LANES), F32),
                   jax.ShapeDtypeStruct((m_rows, 3 * MOBA_WIDTH), BF16),
                   jax.ShapeDtypeStruct((m_rows, 2 * D_MODEL), BF16)],
        scratch_shapes=[pltpu.VMEM((tm + SUBLANES, PROJ_CHUNK), F32)] * n_chunks,
        compiler_params=pltpu.CompilerParams(dimension_semantics=("arbitrary",),
                                             vmem_limit_bytes=VMEM_LIMIT_BYTES),
        name="in_proj",
    )(x2d, wqk, bqk, conv_qk, wvo, bvo, wif, bif, wb, bb, wg, bg)


def _log_sigmoid(x):
    return jnp.minimum(x, 0.0) - jnp.log1p(jnp.exp(-jnp.abs(x)))


def _mlstm_kernel(qk_ref, vo_ref, ift_ref, gain_ref, ya_ref, *state_refs):
    c = pl.program_id(1)
    L = MLSTM_CHUNK
    nh = MLSTM_HEADS
    dh = MLSTM_HEAD_DIM
    c_refs, n_refs, m_refs = state_refs[:nh], state_refs[nh:2 * nh], state_refs[2 * nh:]

    @pl.when(c == 0)
    def _():
        for ref in state_refs:
            ref[...] = jnp.zeros_like(ref)

    rows = ift_ref[0]
    lane = lax.broadcasted_iota(jnp.int32, rows.shape, 1)
    cum = _log_sigmoid(rows)
    shift = 1
    while shift < L:
        cum = cum + jnp.where(lane >= shift, pltpu.roll(cum, shift, axis=1), 0.0)
        shift *= 2

    ri = lax.broadcasted_iota(jnp.int32, (L, L), 0)
    ci = lax.broadcasted_iota(jnp.int32, (L, L), 1)
    eye = ri == ci
    causal = ri >= ci

    for h in range(nh):
        c_ref, n_ref, m_ref = c_refs[h], n_refs[h], m_refs[h]
        hs = slice(h * dh, (h + 1) * dh)
        li_row = rows[h:h + 1, :]
        b_row = cum[nh + h:nh + h + 1, :]
        b_col = jnp.sum(jnp.where(eye, b_row, 0.0), axis=1, keepdims=True)
        li_col = jnp.sum(jnp.where(eye, li_row, 0.0), axis=1, keepdims=True)

        m_prev = m_ref[...]
        d_intra = jnp.where(causal, b_col - b_row + li_row, -jnp.inf)
        inter = b_col + m_prev
        m_q = jnp.maximum(inter, jnp.max(d_intra, axis=1, keepdims=True))
        w_inter = jnp.exp(inter - m_q)

        qb = qk_ref[:, hs]
        kb = qk_ref[:, MLSTM_WIDTH + h * dh:MLSTM_WIDTH + (h + 1) * dh]
        vb = vo_ref[:, hs]
        s = _dot_nt(qb, kb) * jnp.exp(d_intra - m_q)
        num = w_inter * _dot(qb, c_ref[...].astype(BF16)) + _dot(s.astype(BF16), vb)
        qn = jnp.sum(qb.astype(F32) * n_ref[...], axis=1, keepdims=True)
        den = w_inter * qn + jnp.sum(s, axis=1, keepdims=True)
        hh = num / jnp.maximum(jnp.abs(den), jnp.exp(-m_q))

        mu = jnp.mean(hh, axis=1, keepdims=True)
        hc = hh - mu
        var = jnp.mean(hc * hc, axis=1, keepdims=True)
        hn = hc * lax.rsqrt(var + LN_EPS) * gain_ref[:, hs]
        og = vo_ref[:, MLSTM_WIDTH + h * dh:MLSTM_WIDTH + (h + 1) * dh].astype(F32)
        ya_ref[:, hs] = (jax.nn.sigmoid(og) * hn).astype(ya_ref.dtype)

        b_last = b_row[:, L - 1:L]
        d_state_row = b_last - b_row + li_row
        m_new = jnp.maximum(b_last + m_prev, jnp.max(d_state_row, axis=1, keepdims=True))
        w_prev = jnp.exp(b_last + m_prev - m_new)
        w_k = jnp.exp(b_last - b_col + li_col - m_new)
        kw = kb.astype(F32) * w_k
        c_ref[...] = w_prev * c_ref[...] + _dot(kw.T.astype(BF16), vb)
        n_ref[...] = w_prev * n_ref[...] + jnp.sum(kw, axis=0, keepdims=True)
        m_ref[...] = m_new


def _mlstm(qk, vo, ift, gain, batch, seq_len):
    L = MLSTM_CHUNK
    dh = MLSTM_HEAD_DIM
    nh = MLSTM_HEADS
    nc = seq_len // L
    m_rows = qk.shape[0]
    rows = lambda n: pl.BlockSpec((L, n), lambda b, c: (b * nc + c, 0))
    return pl.pallas_call(
        _mlstm_kernel,
        grid=(batch, nc),
        in_specs=[rows(2 * MLSTM_WIDTH), rows(2 * MLSTM_WIDTH),
                  pl.BlockSpec((1, 2 * nh, L), lambda b, c: (b, 0, c)),
                  pl.BlockSpec((1, MLSTM_WIDTH), lambda b, c: (0, 0))],
        out_specs=rows(MLSTM_WIDTH),
        out_shape=jax.ShapeDtypeStruct((m_rows, MLSTM_WIDTH), BF16),
        scratch_shapes=([pltpu.VMEM((dh, dh), F32)] * nh + [pltpu.VMEM((1, dh), F32)] * nh
                        + [pltpu.VMEM((1, 1), F32)] * nh),
        compiler_params=pltpu.CompilerParams(dimension_semantics=("arbitrary", "arbitrary"),
                                             vmem_limit_bytes=VMEM_LIMIT_BYTES),
        name="mlstm",
    )(qk, vo, ift, gain)


def _t5_bucket(rel):
    n = jnp.maximum(rel, 0)
    max_exact = NUM_BUCKETS // 2
    nf = jnp.maximum(n, max_exact).astype(F32)
    large = max_exact + (jnp.log(nf / max_exact) / math.log(REL_MAX_DISTANCE / max_exact)
                         * (NUM_BUCKETS - max_exact)).astype(jnp.int32)
    large = jnp.minimum(large, NUM_BUCKETS - 1)
    return jnp.where(n < max_exact, n, large)


def _moba_kernel(rb_ref, q_ref, k_ref, v_ref, bkt_own_ref, bkt_prev_ref, yb_ref, *scratch, nb):
    hp = MOBA_HEADS_PER_STEP
    per_head = len(scratch) // hp
    step = pl.program_id(2)
    blk = MOBA_BLOCK
    dh = MOBA_HEAD_DIM
    nq = MOBA_QBLOCKS * blk
    G = MOBA_GROUP
    i_lo = MOBA_QBLOCKS * step
    i_hi = i_lo + 1
    lane = lax.broadcasted_iota(jnp.int32, (1, nq), 1)
    n_far = (i_hi + G) // G - 1

    class Head:
        def __init__(self, hh):
            (self.kmean_ref, self.bias_ref, self.mask_ref, self.sa_ref, self.sb_ref,
             self.m_ref, self.l_ref, self.acc_ref) = scratch[hh * per_head:(hh + 1) * per_head]
            self.cs = slice(hh * dh, (hh + 1) * dh)
            self.h = hp * pl.program_id(1) + hh
            self.far_bias = rb_ref[NUM_BUCKETS - 1, self.h] * LOG2E

        def prepare(self):
            def mean_body(j, carry):
                kj = k_ref[0, pl.ds(pl.multiple_of(j * blk, blk), blk), self.cs].astype(F32)
                self.kmean_ref[pl.ds(j, 1), :] = jnp.sum(kj, axis=0, keepdims=True) * (1.0 / blk)
                return carry
            lax.fori_loop(0, nb, mean_body, 0)
            bo = bkt_own_ref[...]
            bp = bkt_prev_ref[...]
            own = jnp.full(bo.shape, NEG, F32)
            prev = jnp.zeros(bp.shape, F32)
            for bucket in range(NUM_BUCKETS):
                val = rb_ref[bucket, self.h] * LOG2E
                own = jnp.where(bo == bucket, val, own)
                prev = jnp.where(bp == bucket, val, prev)
            self.bias_ref[0, :, 0:blk] = jnp.full((blk, blk), NEG, F32)
            self.bias_ref[0, :, blk:nq] = own
            self.bias_ref[1, :, 0:blk] = own
            self.bias_ref[1, :, blk:nq] = prev
            self.bias_ref[2, :, 0:blk] = prev
            self.bias_ref[2, :, blk:nq] = jnp.full((blk, blk), self.far_bias, F32)

        def choose_blocks(self):
            qb = q_ref[:, self.cs]
            km = self.kmean_ref[...]
            km_hi = km.astype(BF16)
            r1 = km - km_hi.astype(F32)
            km_mid = r1.astype(BF16)
            km_lo = (r1 - km_mid.astype(F32)).astype(BF16)
            gate = _dot_nt(km_hi, qb) + _dot_nt(km_mid, qb) + _dot_nt(km_lo, qb)
            row = lax.broadcasted_iota(jnp.int32, gate.shape, 0)
            own_blk = jnp.where(lane < blk, i_lo, i_hi)
            gate = jnp.where(row < own_blk, gate, -jnp.inf)
            chosen = jnp.zeros(gate.shape, jnp.bool_)
            for r in range(MOBA_TOPK):
                mx = jnp.max(gate, axis=0, keepdims=True)
                first = jnp.min(jnp.where(gate == mx, row, nb), axis=0, keepdims=True)
                pick = row == first + jnp.where(own_blk > r, 0, 2 * nb)
                chosen = jnp.logical_or(chosen, pick)
                gate = jnp.where(pick, -jnp.inf, gate)
            self.mask_ref[0:MASK_PAD, :] = jnp.full((MASK_PAD, nq), NEG, F32)
            self.mask_ref[MASK_PAD:MASK_PAD + nb, :] = jnp.where(chosen, 0.0, NEG)

        def mask_row(self, j):
            return self.mask_ref[pl.ds(j + MASK_PAD, 1), :]

        def logits_group(self, t, s_ref):
            qb = q_ref[:, self.cs]
            for r in range(G):
                jc = jnp.maximum(i_hi - G * t - r, 0)
                kj = k_ref[0, pl.ds(pl.multiple_of(jc * blk, blk), blk), self.cs]
                s_ref[r * blk:(r + 1) * blk, :] = _dot_nt(kj, qb)

        def softmax_pv(self, t, s_ref, rows, m_old):
            cmax = [jnp.max(s_ref[r * blk:(r + 1) * blk, :], axis=0, keepdims=True) + rows[r] for r in range(G)]
            m_new = functools.reduce(jnp.maximum, cmax)
            if m_old is not None:
                m_new = jnp.maximum(m_new, m_old)
            lsum = jnp.zeros((1, nq), F32)
            pv = jnp.zeros((dh, nq), F32)
            for r in range(G):
                p = jnp.exp2(s_ref[r * blk:(r + 1) * blk, :] - (m_new - rows[r]))
                lsum = lsum + jnp.sum(p, axis=0, keepdims=True)
                jc = jnp.maximum(i_hi - G * t - r, 0)
                vj = v_ref[0, pl.ds(pl.multiple_of(jc * blk, blk), blk), self.cs]
                pv = pv + _dot_tn(vj, p.astype(BF16))
            return m_new, lsum, pv

        def first_group(self):
            sa_ref = self.sa_ref
            for r in range(3):
                sa_ref[r * blk:(r + 1) * blk, :] = sa_ref[r * blk:(r + 1) * blk, :] + self.bias_ref[r]
            rows0 = [jnp.zeros((1, nq), F32),
                     jnp.where(lane < blk, 0.0, self.mask_row(i_lo)),
                     self.mask_row(i_lo - 1)] + [self.mask_row(i_hi - r) + self.far_bias for r in range(3, G)]
            m0, l0, pv0 = self.softmax_pv(0, sa_ref, rows0, None)
            self.m_ref[...] = m0
            self.l_ref[...] = l0
            self.acc_ref[...] = pv0

        def far_step(self, t, s_ref, next_ref):
            if next_ref is not None:
                self.logits_group(t + 1, next_ref)
            rows = [self.mask_row(i_hi - G * t - r) + self.far_bias for r in range(G)]
            m_old = self.m_ref[...]
            m_new, lsum, pv = self.softmax_pv(t, s_ref, rows, m_old)
            alpha = jnp.exp2(m_old - m_new)
            self.l_ref[...] = alpha * self.l_ref[...] + lsum
            self.acc_ref[...] = alpha * self.acc_ref[...] + pv
            self.m_ref[...] = m_new

        def finish(self):
            out_t = self.acc_ref[...] / self.l_ref[...]
            yb_ref[:, self.cs] = out_t.T.astype(yb_ref.dtype)

    heads = [Head(hh) for hh in range(hp)]

    @pl.when(step == 0)
    def _():
        for hd in heads:
            hd.prepare()

    for hd in heads:
        hd.choose_blocks()
        hd.logits_group(0, hd.sa_ref)
        hd.logits_group(1, hd.sb_ref)
    for hd in heads:
        hd.first_group()

    def pair_body(u, carry):
        for hd in heads:
            hd.far_step(2 * u + 1, hd.sb_ref, hd.sa_ref)
        for hd in heads:
            hd.far_step(2 * u + 2, hd.sa_ref, hd.sb_ref)
        return carry
    lax.fori_loop(0, n_far // 2, pair_body, 0)

    @pl.when(n_far % 2 == 1)
    def _():
        for hd in heads:
            hd.far_step(n_far, hd.sb_ref, None)

    for hd in heads:
        hd.finish()


def _moba(qkvb, rel_bias, batch, seq_len):
    blk = MOBA_BLOCK
    dh = MOBA_HEAD_DIM
    hp = MOBA_HEADS_PER_STEP
    ng = MOBA_HEADS // hp
    nb = seq_len // blk
    nq = MOBA_QBLOCKS * blk
    n_steps = nb // MOBA_QBLOCKS
    m_rows = qkvb.shape[0]
    qkvb3 = qkvb.reshape(batch, seq_len, 3 * MOBA_WIDTH)
    pos = jnp.arange(blk, dtype=jnp.int32)
    rel_own = pos[None, :] - pos[:, None]
    bkt_own = jnp.where(rel_own >= 0, _t5_bucket(rel_own), -1)
    bkt_prev = _t5_bucket(rel_own + blk)
    full2 = lambda shape: pl.BlockSpec(shape, lambda b, g, i: (0, 0))
    per_head_scratch = [pltpu.VMEM((nb, dh), F32),
                        pltpu.VMEM((3, blk, nq), F32),
                        pltpu.VMEM((MASK_PAD + nb, nq), F32),
                        pltpu.VMEM((MOBA_GROUP * blk, nq), F32), pltpu.VMEM((MOBA_GROUP * blk, nq), F32),
                        pltpu.VMEM((1, nq), F32), pltpu.VMEM((1, nq), F32), pltpu.VMEM((dh, nq), F32)]
    return pl.pallas_call(
        functools.partial(_moba_kernel, nb=nb),
        grid=(batch, ng, n_steps),
        in_specs=[pl.BlockSpec(memory_space=pltpu.SMEM),
                  pl.BlockSpec((nq, hp * dh), lambda b, g, i: (b * n_steps + i, g)),
                  pl.BlockSpec((1, seq_len, hp * dh), lambda b, g, i: (b, 0, ng + g)),
                  pl.BlockSpec((1, seq_len, hp * dh), lambda b, g, i: (b, 0, 2 * ng + g)),
                  full2((blk, blk)), full2((blk, blk))],
        out_specs=pl.BlockSpec((nq, hp * dh), lambda b, g, i: (b * n_steps + i, g)),
        out_shape=jax.ShapeDtypeStruct((m_rows, MOBA_WIDTH), BF16),
        scratch_shapes=per_head_scratch * hp,
        compiler_params=pltpu.CompilerParams(dimension_semantics=("arbitrary", "arbitrary", "arbitrary"),
                                             vmem_limit_bytes=VMEM_LIMIT_BYTES),
        name="moba",
    )(rel_bias, qkvb, qkvb3, qkvb3, bkt_own, bkt_prev)


def _mixtail_kernel(x_ref, ya_ref, yb_ref, ga_ref, gb_ref, wa_ref, wb_ref, wo_ref, g_ref, b_ref, out_ref):
    pa = _dot(ya_ref[...], wa_ref[...])
    pb = _dot(yb_ref[...], wb_ref[...])
    merged = (jax.nn.sigmoid(ga_ref[...].astype(F32)) * pa
              + jax.nn.sigmoid(gb_ref[...].astype(F32)) * pb)
    mix = _dot(merged.astype(BF16), wo_ref[...])
    z = ALPHA * x_ref[...] + mix
    out_ref[...] = _layer_norm_rows(z, g_ref[...], b_ref[...])


def _mix_tail(x2d, ya, yb, gg, w_a, w_b, w_o, ln_g, ln_b, tm=512):
    m_rows = x2d.shape[0]
    row = lambda j: pl.BlockSpec((tm, D_MODEL), lambda i: (i, j))
    wa = w_a.astype(BF16)
    wb = w_b.astype(BF16)
    wo = w_o.astype(BF16)
    return pl.pallas_call(
        _mixtail_kernel,
        grid=(m_rows // tm,),
        in_specs=[row(0), row(0), row(0), row(0), row(1),
                  _resident(wa.shape), _resident(wb.shape), _resident(wo.shape),
                  _resident((1, D_MODEL)), _resident((1, D_MODEL))],
        out_specs=row(0),
        out_shape=jax.ShapeDtypeStruct((m_rows, D_MODEL), F32),
        compiler_params=pltpu.CompilerParams(dimension_semantics=("arbitrary",),
                                             vmem_limit_bytes=VMEM_LIMIT_BYTES),
        name="mix_tail",
    )(x2d, ya, yb, gg, gg, wa, wb, wo, ln_g[None, :], ln_b[None, :])


def _ffn_kernel(x_ref, wup_ref, cw_ref, wdn_ref, g_ref, b_ref, out_ref,
                xb_ref, ua_ref, ub_ref, carry_ref, acc_ref, *, tm, tiles_per_seq):
    i = pl.program_id(0)
    halo = FFN_CONV - 1
    xb_ref[...] = x_ref[...].astype(BF16)
    acc_ref[...] = jnp.zeros_like(acc_ref)

    @pl.when(i == 0)
    def _():
        carry_ref[...] = jnp.zeros_like(carry_ref)

    at_start = jnp.full((SUBLANES, 2 * FFN_CHUNK), i % tiles_per_seq, jnp.int32) == 0

    def up_proj(c, u_ref):
        u = _dot(xb_ref[...], wup_ref[c])
        u_ref[0:SUBLANES, :] = jnp.where(at_start, 0.0, carry_ref[c])
        u_ref[SUBLANES:SUBLANES + tm, :] = u
        carry_ref[c] = u[tm - SUBLANES:tm, :]

    def act_down(c, u_ref):
        cw = cw_ref[c]
        conv = cw[halo:halo + 1, :] * u_ref[SUBLANES:SUBLANES + tm, :]
        for k in range(halo):
            off = SUBLANES - halo + k
            conv = conv + cw[k:k + 1, :] * u_ref[off:off + tm, :]
        gate = conv[:, :FFN_CHUNK]
        up = conv[:, FFN_CHUNK:]
        act = (gate * jax.nn.sigmoid(gate) * up).astype(BF16)
        acc_ref[...] += _dot(act, wdn_ref[c])

    up_proj(0, ua_ref)
    for p in range(N_FFN_CHUNKS // 2):
        up_proj(2 * p + 1, ub_ref)
        act_down(2 * p, ua_ref)
        up_proj(2 * p + 2, ua_ref)
        act_down(2 * p + 1, ub_ref)
    act_down(N_FFN_CHUNKS - 1, ua_ref)

    z = ALPHA * x_ref[...] + acc_ref[...]
    out_ref[...] = _layer_norm_rows(z, g_ref[...], b_ref[...])


def _ffn(x2d, w_up, conv_ffn, w_down, ln_g, ln_b, seq_len, tm=512):
    m_rows = x2d.shape[0]
    nch = N_FFN_CHUNKS
    fc = FFN_CHUNK

    def pair(t):
        lead = t.shape[:-1]
        t = t.reshape(*lead, 2, nch, fc)
        t = jnp.moveaxis(t, -2, 0)
        return t.reshape(nch, *lead, 2 * fc)

    wup = pair(w_up).astype(BF16)
    cw = pair(conv_ffn)
    wdn = w_down.reshape(nch, fc, D_MODEL).astype(BF16)
    row = pl.BlockSpec((tm, D_MODEL), lambda i: (i, 0))
    return pl.pallas_call(
        functools.partial(_ffn_kernel, tm=tm, tiles_per_seq=seq_len // tm),
        grid=(m_rows // tm,),
        in_specs=[row, _resident(wup.shape), _resident(cw.shape), _resident(wdn.shape),
                  _resident((1, D_MODEL)), _resident((1, D_MODEL))],
        out_specs=row,
        out_shape=jax.ShapeDtypeStruct((m_rows, D_MODEL), F32),
        scratch_shapes=[pltpu.VMEM((tm, D_MODEL), BF16),
                        pltpu.VMEM((tm + SUBLANES, 2 * fc), F32), pltpu.VMEM((tm + SUBLANES, 2 * fc), F32),
                        pltpu.VMEM((nch, SUBLANES, 2 * fc), F32),
                        pltpu.VMEM((tm, D_MODEL), F32)],
        compiler_params=pltpu.CompilerParams(dimension_semantics=("arbitrary",),
                                             vmem_limit_bytes=VMEM_LIMIT_BYTES),
        name="ffn",
    )(x2d, wup, cw, wdn, ln_g[None, :], ln_b[None, :])


def kernel(x, w_in, b_in, conv_qk, mlstm_norm, rel_bias, w_branch_a, w_branch_b, w_out,
           ln1_g, ln1_b, w_up, conv_ffn, w_down, ln2_g, ln2_b):
    batch, seq_len, d_model = x.shape
    assert d_model == D_MODEL
    assert seq_len % (MOBA_GROUP * MOBA_BLOCK) == 0
    h = x.reshape(batch * seq_len, d_model)
    for l in range(DEPTH):
        qk, vo, ifp, qkvb, gg = _in_proj(h, w_in[l], b_in[l], conv_qk[l], seq_len)
        ift = ifp[:, :2 * MLSTM_HEADS].reshape(batch, seq_len, 2 * MLSTM_HEADS).transpose(0, 2, 1)
        ya = _mlstm(qk, vo, ift, mlstm_norm[l][None, :], batch, seq_len)
        yb = _moba(qkvb, rel_bias, batch, seq_len)
        h = _mix_tail(h, ya, yb, gg, w_branch_a[l], w_branch_b[l], w_out[l], ln1_g[l], ln1_b[l])
        h = _ffn(h, w_up[l], conv_ffn[l], w_down[l], ln2_g[l], ln2_b[l], seq_len)
    return h.reshape(batch, seq_len, d_model)
```

```python
import functools
import math

import jax
import jax.numpy as jnp
from jax import lax
from jax.experimental import pallas as pl
from jax.experimental.pallas import tpu as pltpu

D_MODEL = 1024
DEPTH = 2
MLSTM_HEADS = 4
MLSTM_HEAD_DIM = 256
MLSTM_WIDTH = MLSTM_HEADS * MLSTM_HEAD_DIM
MLSTM_CONV = 4
MOBA_HEADS = 8
MOBA_HEAD_DIM = 128
MOBA_WIDTH = MOBA_HEADS * MOBA_HEAD_DIM
MOBA_BLOCK = 256
MOBA_TOPK = 3
NUM_BUCKETS = 32
REL_MAX_DISTANCE = 128
D_FF = 2816
FFN_CONV = 3
ALPHA = (2 * DEPTH) ** 0.25
LN_EPS = 1e-5

SUBLANES = 8
LANES = 128
VMEM_LIMIT_BYTES = 56 * 1024 * 1024

MLSTM_CHUNK = 256
MLSTM_SEQS_PER_STEP = 2
FFN_CHUNK = 256
N_FFN_CHUNKS = D_FF // FFN_CHUNK
PROJ_CHUNK = 512
NEG = -1e30
MOBA_GROUP = 4
MOBA_QBLOCKS = 2
MOBA_HEADS_PER_STEP = 2
LOG2E = math.log2(math.e)
MASK_PAD = SUBLANES

BF16 = jnp.bfloat16
F32 = jnp.float32


def _dot(a, b):
    return jnp.dot(a, b, preferred_element_type=F32)


def _dot_nt(a, b):
    return lax.dot_general(a, b, (((1,), (1,)), ((), ())), preferred_element_type=F32)


def _dot_tn(a, b):
    return lax.dot_general(a, b, (((0,), (0,)), ((), ())), preferred_element_type=F32)


def _resident(shape):
    nd = len(shape)
    return pl.BlockSpec(shape, lambda *_: (0,) * nd, pipeline_mode=pl.Buffered(1))


def _layer_norm_rows(z, g, b):
    mu = jnp.mean(z, axis=-1, keepdims=True)
    zc = z - mu
    var = jnp.mean(zc * zc, axis=-1, keepdims=True)
    return zc * lax.rsqrt(var + LN_EPS) * g + b


def _inproj_kernel(x_ref, wqk_ref, bqk_ref, cw_ref, wvo_ref, bvo_ref, wif_ref, bif_ref,
                   wb_ref, bb_ref, wg_ref, bg_ref,
                   qk_ref, vo_ref, if_ref, qkvb_ref, gg_ref, *ubuf_refs, tm, tiles_per_seq):
    i = pl.program_id(0)
    xb = x_ref[...].astype(BF16)
    halo = MLSTM_CONV - 1

    @pl.when(i == 0)
    def _():
        for ubuf_ref in ubuf_refs:
            ubuf_ref[tm:tm + SUBLANES, :] = jnp.zeros((SUBLANES, PROJ_CHUNK), F32)

    at_start = jnp.full((SUBLANES, PROJ_CHUNK), i % tiles_per_seq, jnp.int32) == 0

    def conv_chunk(c):
        ubuf_ref = ubuf_refs[c]
        cs = slice(c * PROJ_CHUNK, (c + 1) * PROJ_CHUNK)
        u = _dot(xb, wqk_ref[:, cs]) + bqk_ref[:, cs]
        ubuf_ref[0:SUBLANES, :] = jnp.where(at_start, 0.0, ubuf_ref[tm:tm + SUBLANES, :])
        ubuf_ref[SUBLANES:SUBLANES + tm, :] = u
        conv = cw_ref[halo:halo + 1, cs] * u
        for k in range(halo):
            off = SUBLANES - halo + k
            conv = conv + cw_ref[k:k + 1, cs] * ubuf_ref[off:off + tm, :]
        act = conv * jax.nn.sigmoid(conv)
        if c * PROJ_CHUNK >= MLSTM_WIDTH:
            act = act * (MLSTM_HEAD_DIM ** -0.5)
        qk_ref[:, cs] = act.astype(qk_ref.dtype)

    def plain_chunk(w_ref, b_ref, o_ref, c):
        cs = slice(c * PROJ_CHUNK, (c + 1) * PROJ_CHUNK)
        o_ref[:, cs] = (_dot(xb, w_ref[:, cs]) + b_ref[:, cs]).astype(o_ref.dtype)

    plain = ([(wvo_ref, bvo_ref, vo_ref, c) for c in range(2 * MLSTM_WIDTH // PROJ_CHUNK)]
             + [(wb_ref, bb_ref, qkvb_ref, c) for c in range(3 * MOBA_WIDTH // PROJ_CHUNK)]
             + [(wg_ref, bg_ref, gg_ref, c) for c in range(2 * D_MODEL // PROJ_CHUNK)])
    per_conv = -(-len(plain) // len(ubuf_refs))
    for c in range(len(ubuf_refs)):
        conv_chunk(c)
        for job in plain[c * per_conv:(c + 1) * per_conv]:
            plain_chunk(*job)
    if_ref[...] = _dot(xb, wif_ref[...]) + bif_ref[...]


def _in_proj(x2d, w_in, b_in, conv_qk, seq_len, tm=512):
    m_rows = x2d.shape[0]
    w4 = 4 * MLSTM_WIDTH
    nh = MLSTM_HEADS
    b0 = w4 + 2 * nh
    scale_b = MOBA_HEAD_DIM ** -0.5 * LOG2E
    col_scale = jnp.concatenate([jnp.full((MOBA_WIDTH,), scale_b, F32), jnp.ones((2 * MOBA_WIDTH,), F32)])
    wqk = w_in[:, :2 * MLSTM_WIDTH].astype(BF16)
    bqk = b_in[None, :2 * MLSTM_WIDTH]
    wvo = w_in[:, 2 * MLSTM_WIDTH:w4].astype(BF16)
    bvo = b_in[None, 2 * MLSTM_WIDTH:w4]
    wif = jnp.pad(w_in[:, w4:b0], ((0, 0), (0, LANES - 2 * nh))).astype(BF16)
    bif = jnp.pad(b_in[w4:b0], (0, LANES - 2 * nh))[None, :]
    wb = (w_in[:, b0:b0 + 3 * MOBA_WIDTH] * col_scale).astype(BF16)
    bb = (b_in[b0:b0 + 3 * MOBA_WIDTH] * col_scale)[None, :]
    wg = w_in[:, b0 + 3 * MOBA_WIDTH:].astype(BF16)
    bg = b_in[None, b0 + 3 * MOBA_WIDTH:]

    row = lambda n: pl.BlockSpec((tm, n), lambda i: (i, 0))
    n_chunks = 2 * MLSTM_WIDTH // PROJ_CHUNK
    return pl.pallas_call(
        functools.partial(_inproj_kernel, tm=tm, tiles_per_seq=seq_len // tm),
        grid=(m_rows // tm,),
        in_specs=[row(D_MODEL),
                  _resident(wqk.shape), _resident(bqk.shape), _resident(conv_qk.shape),
                  _resident(wvo.shape), _resident(bvo.shape),
                  _resident(wif.shape), _resident(bif.shape),
                  _resident(wb.shape), _resident(bb.shape),
                  _resident(wg.shape), _resident(bg.shape)],
        out_specs=[row(2 * MLSTM_WIDTH), row(2 * MLSTM_WIDTH), row(LANES), row(3 * MOBA_WIDTH), row(2 * D_MODEL)],
        out_shape=[jax.ShapeDtypeStruct((m_rows, 2 * MLSTM_WIDTH), BF16),
                   jax.ShapeDtypeStruct((m_rows, 2 * MLSTM_WIDTH), BF16),
                   jax.ShapeDtypeStruct((m_rows, LANES), F32),
                   jax.ShapeDtypeStruct((m_rows, 3 * MOBA_WIDTH), BF16),
                   jax.ShapeDtypeStruct((m_rows, 2 * D_MODEL), BF16)],
        scratch_shapes=[pltpu.VMEM((tm + SUBLANES, PROJ_CHUNK), F32)] * n_chunks,
        compiler_params=pltpu.CompilerParams(dimension_semantics=("arbitrary",),
                                             vmem_limit_bytes=VMEM_LIMIT_BYTES),
        name="in_proj",
    )(x2d, wqk, bqk, conv_qk, wvo, bvo, wif, bif, wb, bb, wg, bg)


def _log_sigmoid(x):
    return jnp.minimum(x, 0.0) - jnp.log1p(jnp.exp(-jnp.abs(x)))


def _mlstm_kernel(qk_refs, vo_refs, ift_ref, gain_ref, ya_refs, *state_refs):
    c = pl.program_id(1)
    L = MLSTM_CHUNK
    nh = MLSTM_HEADS
    dh = MLSTM_HEAD_DIM
    n_chains = MLSTM_SEQS_PER_STEP * nh
    c_refs, n_refs, m_refs = state_refs[:n_chains], state_refs[n_chains:2 * n_chains], state_refs[2 * n_chains:]

    @pl.when(c == 0)
    def _():
        for ref in state_refs:
            ref[...] = jnp.zeros_like(ref)

    ri = lax.broadcasted_iota(jnp.int32, (L, L), 0)
    ci = lax.broadcasted_iota(jnp.int32, (L, L), 1)
    eye = ri == ci
    causal = ri >= ci
    lane = lax.broadcasted_iota(jnp.int32, (2 * nh, L), 1)

    gates = []
    for bb in range(MLSTM_SEQS_PER_STEP):
        rows = ift_ref[bb]
        cum = _log_sigmoid(rows)
        shift = 1
        while shift < L:
            cum = cum + jnp.where(lane >= shift, pltpu.roll(cum, shift, axis=1), 0.0)
            shift *= 2
        gates.append((rows, cum))

    for chain in range(n_chains):
        bb, h = divmod(chain, nh)
        rows, cum = gates[bb]
        qk_ref, vo_ref, ya_ref = qk_refs.at[bb], vo_refs.at[bb], ya_refs.at[bb]
        c_ref, n_ref, m_ref = c_refs[chain], n_refs[chain], m_refs[chain]
        hs = slice(h * dh, (h + 1) * dh)
        li_row = rows[h:h + 1, :]
        b_row = cum[nh + h:nh + h + 1, :]
        b_col = jnp.sum(jnp.where(eye, b_row, 0.0), axis=1, keepdims=True)
        li_col = jnp.sum(jnp.where(eye, li_row, 0.0), axis=1, keepdims=True)

        m_prev = m_ref[...]
        d_intra = jnp.where(causal, b_col - b_row + li_row, -jnp.inf)
        inter = b_col + m_prev
        m_q = jnp.maximum(inter, jnp.max(d_intra, axis=1, keepdims=True))
        w_inter = jnp.exp(inter - m_q)

        qb = qk_ref[:, hs]
        kb = qk_ref[:, MLSTM_WIDTH + h * dh:MLSTM_WIDTH + (h + 1) * dh]
        vb = vo_ref[:, hs]
        s = _dot_nt(qb, kb) * jnp.exp(d_intra - m_q)
        num = w_inter * _dot(qb, c_ref[...].astype(BF16)) + _dot(s.astype(BF16), vb)
        qn = jnp.sum(qb.astype(F32) * n_ref[...], axis=1, keepdims=True)
        den = w_inter * qn + jnp.sum(s, axis=1, keepdims=True)
        hh = num / jnp.maximum(jnp.abs(den), jnp.exp(-m_q))

        mu = jnp.mean(hh, axis=1, keepdims=True)
        hc = hh - mu
        var = jnp.mean(hc * hc, axis=1, keepdims=True)
        hn = hc * lax.rsqrt(var + LN_EPS) * gain_ref[:, hs]
        og = vo_ref[:, MLSTM_WIDTH + h * dh:MLSTM_WIDTH + (h + 1) * dh].astype(F32)
        ya_ref[:, hs] = (jax.nn.sigmoid(og) * hn).astype(ya_ref.dtype)

        b_last = b_row[:, L - 1:L]
        d_state_row = b_last - b_row + li_row
        m_new = jnp.maximum(b_last + m_prev, jnp.max(d_state_row, axis=1, keepdims=True))
        w_prev = jnp.exp(b_last + m_prev - m_new)
        w_k = jnp.exp(b_last - b_col + li_col - m_new)
        kw = kb.astype(F32) * w_k
        c_ref[...] = w_prev * c_ref[...] + _dot(kw.T.astype(BF16), vb)
        n_ref[...] = w_prev * n_ref[...] + jnp.sum(kw, axis=0, keepdims=True)
        m_ref[...] = m_new


def _mlstm(qk, vo, ift, gain, batch, seq_len):
    L = MLSTM_CHUNK
    dh = MLSTM_HEAD_DIM
    nh = MLSTM_HEADS
    nc = seq_len // L
    m_rows = qk.shape[0]
    ns = MLSTM_SEQS_PER_STEP
    n_chains = ns * nh
    rows = lambda n: pl.BlockSpec((ns, L, n), lambda b, c: (b, c, 0))
    ya = pl.pallas_call(
        _mlstm_kernel,
        grid=(batch // ns, nc),
        in_specs=[rows(2 * MLSTM_WIDTH), rows(2 * MLSTM_WIDTH),
                  pl.BlockSpec((ns, 2 * nh, L), lambda b, c: (b, 0, c)),
                  pl.BlockSpec((1, MLSTM_WIDTH), lambda b, c: (0, 0))],
        out_specs=rows(MLSTM_WIDTH),
        out_shape=jax.ShapeDtypeStruct((batch, seq_len, MLSTM_WIDTH), BF16),
        scratch_shapes=([pltpu.VMEM((dh, dh), F32)] * n_chains + [pltpu.VMEM((1, dh), F32)] * n_chains
                        + [pltpu.VMEM((1, 1), F32)] * n_chains),
        compiler_params=pltpu.CompilerParams(dimension_semantics=("arbitrary", "arbitrary"),
                                             vmem_limit_bytes=VMEM_LIMIT_BYTES),
        name="mlstm",
    )(qk.reshape(batch, seq_len, 2 * MLSTM_WIDTH), vo.reshape(batch, seq_len, 2 * MLSTM_WIDTH), ift, gain)
    return ya.reshape(m_rows, MLSTM_WIDTH)


def _t5_bucket(rel):
    n = jnp.maximum(rel, 0)
    max_exact = NUM_BUCKETS // 2
    nf = jnp.maximum(n, max_exact).astype(F32)
    large = max_exact + (jnp.log(nf / max_exact) / math.log(REL_MAX_DISTANCE / max_exact)
                         * (NUM_BUCKETS - max_exact)).astype(jnp.int32)
    large = jnp.minimum(large, NUM_BUCKETS - 1)
    return jnp.where(n < max_exact, n, large)


def _moba_kernel(rb_ref, q_ref, k_ref, v_ref, bkt_own_ref, bkt_prev_ref, yb_ref, *scratch, nb):
    hp = MOBA_HEADS_PER_STEP
    per_head = len(scratch) // hp
    step = pl.program_id(2)
    blk = MOBA_BLOCK
    dh = MOBA_HEAD_DIM
    nq = MOBA_QBLOCKS * blk
    G = MOBA_GROUP
    i_lo = MOBA_QBLOCKS * step
    i_hi = i_lo + 1
    lane = lax.broadcasted_iota(jnp.int32, (1, nq), 1)
    n_far = (i_hi + G) // G - 1

    class Head:
        def __init__(self, hh):
            (self.kmean_ref, self.bias_ref, self.mask_ref, self.sa_ref, self.sb_ref,
             self.m_ref, self.l_ref, self.acc_ref) = scratch[hh * per_head:(hh + 1) * per_head]
            self.cs = slice(hh * dh, (hh + 1) * dh)
            self.h = hp * pl.program_id(1) + hh
            self.far_bias = rb_ref[NUM_BUCKETS - 1, self.h] * LOG2E

        def prepare(self):
            def mean_body(j, carry):
                kj = k_ref[0, pl.ds(pl.multiple_of(j * blk, blk), blk), self.cs].astype(F32)
                self.kmean_ref[pl.ds(j, 1), :] = jnp.sum(kj, axis=0, keepdims=True) * (1.0 / blk)
                return carry
            lax.fori_loop(0, nb, mean_body, 0)
            bo = bkt_own_ref[...]
            bp = bkt_prev_ref[...]
            own = jnp.full(bo.shape, NEG, F32)
            prev = jnp.zeros(bp.shape, F32)
            for bucket in range(NUM_BUCKETS):
                val = rb_ref[bucket, self.h] * LOG2E
                own = jnp.where(bo == bucket, val, own)
                prev = jnp.where(bp == bucket, val, prev)
            self.bias_ref[0, :, 0:blk] = jnp.full((blk, blk), NEG, F32)
            self.bias_ref[0, :, blk:nq] = own
            self.bias_ref[1, :, 0:blk] = own
            self.bias_ref[1, :, blk:nq] = prev
            self.bias_ref[2, :, 0:blk] = prev
            self.bias_ref[2, :, blk:nq] = jnp.full((blk, blk), self.far_bias, F32)

        def choose_blocks(self):
            qb = q_ref[:, self.cs]
            km = self.kmean_ref[...]
            km_hi = km.astype(BF16)
            r1 = km - km_hi.astype(F32)
            km_mid = r1.astype(BF16)
            km_lo = (r1 - km_mid.astype(F32)).astype(BF16)
            gate = _dot_nt(km_hi, qb) + _dot_nt(km_mid, qb) + _dot_nt(km_lo, qb)
            row = lax.broadcasted_iota(jnp.int32, gate.shape, 0)
            own_blk = jnp.where(lane < blk, i_lo, i_hi)
            gate = jnp.where(row < own_blk, gate, -jnp.inf)
            chosen = jnp.zeros(gate.shape, jnp.bool_)
            for r in range(MOBA_TOPK):
                mx = jnp.max(gate, axis=0, keepdims=True)
                first = jnp.min(jnp.where(gate == mx, row, nb), axis=0, keepdims=True)
                pick = row == first + jnp.where(own_blk > r, 0, 2 * nb)
                chosen = jnp.logical_or(chosen, pick)
                gate = jnp.where(pick, -jnp.inf, gate)
            self.mask_ref[0:MASK_PAD, :] = jnp.full((MASK_PAD, nq), NEG, F32)
            self.mask_ref[MASK_PAD:MASK_PAD + nb, :] = jnp.where(chosen, 0.0, NEG)

        def mask_row(self, j):
            return self.mask_ref[pl.ds(j + MASK_PAD, 1), :]

        def logits_group(self, t, s_ref):
            qb = q_ref[:, self.cs]
            for r in range(G):
                jc = jnp.maximum(i_hi - G * t - r, 0)
                kj = k_ref[0, pl.ds(pl.multiple_of(jc * blk, blk), blk), self.cs]
                s_ref[r * blk:(r + 1) * blk, :] = _dot_nt(kj, qb)

        def softmax_pv(self, t, s_ref, rows, m_old):
            cmax = [jnp.max(s_ref[r * blk:(r + 1) * blk, :], axis=0, keepdims=True) + rows[r] for r in range(G)]
            m_new = functools.reduce(jnp.maximum, cmax)
            if m_old is not None:
                m_new = jnp.maximum(m_new, m_old)
            lsum = jnp.zeros((1, nq), F32)
            pv = jnp.zeros((dh, nq), F32)
            for r in range(G):
                p = jnp.exp2(s_ref[r * blk:(r + 1) * blk, :] - (m_new - rows[r]))
                lsum = lsum + jnp.sum(p, axis=0, keepdims=True)
                jc = jnp.maximum(i_hi - G * t - r, 0)
                vj = v_ref[0, pl.ds(pl.multiple_of(jc * blk, blk), blk), self.cs]
                pv = pv + _dot_tn(vj, p.astype(BF16))
            return m_new, lsum, pv

        def first_group(self):
            sa_ref = self.sa_ref
            for r in range(3):
                sa_ref[r * blk:(r + 1) * blk, :] = sa_ref[r * blk:(r + 1) * blk, :] + self.bias_ref[r]
            rows0 = [jnp.zeros((1, nq), F32),
                     jnp.where(lane < blk, 0.0, self.mask_row(i_lo)),
                     self.mask_row(i_lo - 1)] + [self.mask_row(i_hi - r) + self.far_bias for r in range(3, G)]
            m0, l0, pv0 = self.softmax_pv(0, sa_ref, rows0, None)
            self.m_ref[...] = m0
            self.l_ref[...] = l0
            self.acc_ref[...] = pv0

        def far_step(self, t, s_ref, next_ref):
            if next_ref is not None:
                self.logits_group(t + 1, next_ref)
            rows = [self.mask_row(i_hi - G * t - r) + self.far_bias for r in range(G)]
            m_old = self.m_ref[...]
            m_new, lsum, pv = self.softmax_pv(t, s_ref, rows, m_old)
            alpha = jnp.exp2(m_old - m_new)
            self.l_ref[...] = alpha * self.l_ref[...] + lsum
            self.acc_ref[...] = alpha * self.acc_ref[...] + pv
            self.m_ref[...] = m_new

        def finish(self):
            out_t = self.acc_ref[...] / self.l_ref[...]
            yb_ref[:, self.cs] = out_t.T.astype(yb_ref.dtype)

    heads = [Head(hh) for hh in range(hp)]

    @pl.when(step == 0)
    def _():
        for hd in heads:
            hd.prepare()

    for hd in heads:
        hd.choose_blocks()
        hd.logits_group(0, hd.sa_ref)
        hd.logits_group(1, hd.sb_ref)
    for hd in heads:
        hd.first_group()

    def pair_body(u, carry):
        for hd in heads:
            hd.far_step(2 * u + 1, hd.sb_ref, hd.sa_ref)
        for hd in heads:
            hd.far_step(2 * u + 2, hd.sa_ref, hd.sb_ref)
        return carry
    lax.fori_loop(0, n_far // 2, pair_body, 0)

    @pl.when(n_far % 2 == 1)
    def _():
        for hd in heads:
            hd.far_step(n_far, hd.sb_ref, None)

    for hd in heads:
        hd.finish()


def _moba(qkvb, rel_bias, batch, seq_len):
    blk = MOBA_BLOCK
    dh = MOBA_HEAD_DIM
    hp = MOBA_HEADS_PER_STEP
    ng = MOBA_HEADS // hp
    nb = seq_len // blk
    nq = MOBA_QBLOCKS * blk
    n_steps = nb // MOBA_QBLOCKS
    m_rows = qkvb.shape[0]
    qkvb3 = qkvb.reshape(batch, seq_len, 3 * MOBA_WIDTH)
    pos = jnp.arange(blk, dtype=jnp.int32)
    rel_own = pos[None, :] - pos[:, None]
    bkt_own = jnp.where(rel_own >= 0, _t5_bucket(rel_own), -1)
    bkt_prev = _t5_bucket(rel_own + blk)
    full2 = lambda shape: pl.BlockSpec(shape, lambda b, g, i: (0, 0))
    per_head_scratch = [pltpu.VMEM((nb, dh), F32),
                        pltpu.VMEM((3, blk, nq), F32),
                        pltpu.VMEM((MASK_PAD + nb, nq), F32),
                        pltpu.VMEM((MOBA_GROUP * blk, nq), F32), pltpu.VMEM((MOBA_GROUP * blk, nq), F32),
                        pltpu.VMEM((1, nq), F32), pltpu.VMEM((1, nq), F32), pltpu.VMEM((dh, nq), F32)]
    return pl.pallas_call(
        functools.partial(_moba_kernel, nb=nb),
        grid=(batch, ng, n_steps),
        in_specs=[pl.BlockSpec(memory_space=pltpu.SMEM),
                  pl.BlockSpec((nq, hp * dh), lambda b, g, i: (b * n_steps + i, g)),
                  pl.BlockSpec((1, seq_len, hp * dh), lambda b, g, i: (b, 0, ng + g)),
                  pl.BlockSpec((1, seq_len, hp * dh), lambda b, g, i: (b, 0, 2 * ng + g)),
                  full2((blk, blk)), full2((blk, blk))],
        out_specs=pl.BlockSpec((nq, hp * dh), lambda b, g, i: (b * n_steps + i, g)),
        out_shape=jax.ShapeDtypeStruct((m_rows, MOBA_WIDTH), BF16),
        scratch_shapes=per_head_scratch * hp,
        compiler_params=pltpu.CompilerParams(dimension_semantics=("arbitrary", "arbitrary", "arbitrary"),
                                             vmem_limit_bytes=VMEM_LIMIT_BYTES),
        name="moba",
    )(rel_bias, qkvb, qkvb3, qkvb3, bkt_own, bkt_prev)


def _mixtail_kernel(x_ref, ya_ref, yb_ref, ga_ref, gb_ref, wa_ref, wb_ref, wo_ref, g_ref, b_ref, out_ref):
    pa = _dot(ya_ref[...], wa_ref[...])
    pb = _dot(yb_ref[...], wb_ref[...])
    merged = (jax.nn.sigmoid(ga_ref[...].astype(F32)) * pa
              + jax.nn.sigmoid(gb_ref[...].astype(F32)) * pb)
    mix = _dot(merged.astype(BF16), wo_ref[...])
    z = ALPHA * x_ref[...] + mix
    out_ref[...] = _layer_norm_rows(z, g_ref[...], b_ref[...])


def _mix_tail(x2d, ya, yb, gg, w_a, w_b, w_o, ln_g, ln_b, tm=512):
    m_rows = x2d.shape[0]
    row = lambda j: pl.BlockSpec((tm, D_MODEL), lambda i: (i, j))
    wa = w_a.astype(BF16)
    wb = w_b.astype(BF16)
    wo = w_o.astype(BF16)
    return pl.pallas_call(
        _mixtail_kernel,
        grid=(m_rows // tm,),
        in_specs=[row(0), row(0), row(0), row(0), row(1),
                  _resident(wa.shape), _resident(wb.shape), _resident(wo.shape),
                  _resident((1, D_MODEL)), _resident((1, D_MODEL))],
        out_specs=row(0),
        out_shape=jax.ShapeDtypeStruct((m_rows, D_MODEL), F32),
        compiler_params=pltpu.CompilerParams(dimension_semantics=("arbitrary",),
                                             vmem_limit_bytes=VMEM_LIMIT_BYTES),
        name="mix_tail",
    )(x2d, ya, yb, gg, gg, wa, wb, wo, ln_g[None, :], ln_b[None, :])


def _ffn_kernel(x_ref, wup_ref, cw_ref, wdn_ref, g_ref, b_ref, out_ref,
                xb_ref, ua_ref, ub_ref, carry_ref, acc_ref, *, tm, tiles_per_seq):
    i = pl.program_id(0)
    halo = FFN_CONV - 1
    xb_ref[...] = x_ref[...].astype(BF16)

    @pl.when(i == 0)
    def _():
        carry_ref[...] = jnp.zeros_like(carry_ref)

    at_start = jnp.full((SUBLANES, 2 * FFN_CHUNK), i % tiles_per_seq, jnp.int32) == 0

    def up_proj(c, u_ref):
        u = _dot(xb_ref[...], wup_ref[c])
        u_ref[0:SUBLANES, :] = jnp.where(at_start, 0.0, carry_ref[c])
        u_ref[SUBLANES:SUBLANES + tm, :] = u
        carry_ref[c] = u[tm - SUBLANES:tm, :]

    def act_down(c, u_ref):
        cw = cw_ref[c]
        conv = cw[halo:halo + 1, :] * u_ref[SUBLANES:SUBLANES + tm, :]
        for k in range(halo):
            off = SUBLANES - halo + k
            conv = conv + cw[k:k + 1, :] * u_ref[off:off + tm, :]
        gate = conv[:, :FFN_CHUNK]
        up = conv[:, FFN_CHUNK:]
        act = (gate * jax.nn.sigmoid(gate) * up).astype(BF16)
        if c == 0:
            acc_ref[...] = _dot(act, wdn_ref[c])
        else:
            acc_ref[...] += _dot(act, wdn_ref[c])

    up_proj(0, ua_ref)
    for p in range(N_FFN_CHUNKS // 2):
        up_proj(2 * p + 1, ub_ref)
        act_down(2 * p, ua_ref)
        up_proj(2 * p + 2, ua_ref)
        act_down(2 * p + 1, ub_ref)
    act_down(N_FFN_CHUNKS - 1, ua_ref)

    z = ALPHA * x_ref[...] + acc_ref[...]
    out_ref[...] = _layer_norm_rows(z, g_ref[...], b_ref[...])


def _ffn(x2d, w_up, conv_ffn, w_down, ln_g, ln_b, seq_len, tm=512):
    m_rows = x2d.shape[0]
    nch = N_FFN_CHUNKS
    fc = FFN_CHUNK

    def pair(t):
        lead = t.shape[:-1]
        t = t.reshape(*lead, 2, nch, fc)
        t = jnp.moveaxis(t, -2, 0)
        return t.reshape(nch, *lead, 2 * fc)

    wup = pair(w_up).astype(BF16)
    cw = pair(conv_ffn)
    wdn = w_down.reshape(nch, fc, D_MODEL).astype(BF16)
    row = pl.BlockSpec((tm, D_MODEL), lambda i: (i, 0))
    return pl.pallas_call(
        functools.partial(_ffn_kernel, tm=tm, tiles_per_seq=seq_len // tm),
        grid=(m_rows // tm,),
        in_specs=[row, _resident(wup.shape), _resident(cw.shape), _resident(wdn.shape),
                  _resident((1, D_MODEL)), _resident((1, D_MODEL))],
        out_specs=row,
        out_shape=jax.ShapeDtypeStruct((m_rows, D_MODEL), F32),
        scratch_shapes=[pltpu.VMEM((tm, D_MODEL), BF16),
                        pltpu.VMEM((tm + SUBLANES, 2 * fc), F32), pltpu.VMEM((tm + SUBLANES, 2 * fc), F32),
                        pltpu.VMEM((nch, SUBLANES, 2 * fc), F32),
                        pltpu.VMEM((tm, D_MODEL), F32)],
        compiler_params=pltpu.CompilerParams(dimension_semantics=("arbitrary",),
                                             vmem_limit_bytes=VMEM_LIMIT_BYTES),
        name="ffn",
    )(x2d, wup, cw, wdn, ln_g[None, :], ln_b[None, :])


def kernel(x, w_in, b_in, conv_qk, mlstm_norm, rel_bias, w_branch_a, w_branch_b, w_out,
           ln1_g, ln1_b, w_up, conv_ffn, w_down, ln2_g, ln2_b):
    batch, seq_len, d_model = x.shape
    assert d_model == D_MODEL
    assert seq_len % (MOBA_GROUP * MOBA_BLOCK) == 0
    assert batch % MLSTM_SEQS_PER_STEP == 0
    h = x.reshape(batch * seq_len, d_model)
    for l in range(DEPTH):
        qk, vo, ifp, qkvb, gg = _in_proj(h, w_in[l], b_in[l], conv_qk[l], seq_len)
        ift = ifp[:, :2 * MLSTM_HEADS].reshape(batch, seq_len, 2 * MLSTM_HEADS).transpose(0, 2, 1)
        ya = _mlstm(qk, vo, ift, mlstm_norm[l][None, :], batch, seq_len)
        yb = _moba(qkvb, rel_bias, batch, seq_len)
        h = _mix_tail(h, ya, yb, gg, w_branch_a[l], w_branch_b[l], w_out[l], ln1_g[l], ln1_b[l])
        h = _ffn(h, w_up[l], conv_ffn[l], w_down[l], ln2_g[l], ln2_b[l], seq_len)
    return h.reshape(batch, seq_len, d_model)
```

```python
import functools
import math

import jax
import jax.numpy as jnp
from jax import lax
from jax.experimental import pallas as pl
from jax.experimental.pallas import tpu as pltpu

D_MODEL = 1024
DEPTH = 2
MLSTM_HEADS = 4
MLSTM_HEAD_DIM = 256
MLSTM_WIDTH = MLSTM_HEADS * MLSTM_HEAD_DIM
MLSTM_CONV = 4
MOBA_HEADS = 8
MOBA_HEAD_DIM = 128
MOBA_WIDTH = MOBA_HEADS * MOBA_HEAD_DIM
MOBA_BLOCK = 256
MOBA_TOPK = 3
NUM_BUCKETS = 32
REL_MAX_DISTANCE = 128
D_FF = 2816
FFN_CONV = 3
ALPHA = (2 * DEPTH) ** 0.25
LN_EPS = 1e-5

SUBLANES = 8
LANES = 128
VMEM_LIMIT_BYTES = 56 * 1024 * 1024

MLSTM_CHUNK = 256
MLSTM_SEQS_PER_STEP = 1
FFN_CHUNK = 256
N_FFN_CHUNKS = D_FF // FFN_CHUNK
PROJ_CHUNK = 512
NEG = -1e30
MOBA_GROUP = 4
MOBA_QBLOCKS = 2
MOBA_HEADS_PER_STEP = 2
LOG2E = math.log2(math.e)
MASK_PAD = SUBLANES

BF16 = jnp.bfloat16
F32 = jnp.float32


def _dot(a, b):
    return jnp.dot(a, b, preferred_element_type=F32)


def _dot_nt(a, b):
    return lax.dot_general(a, b, (((1,), (1,)), ((), ())), preferred_element_type=F32)


def _dot_tn(a, b):
    return lax.dot_general(a, b, (((0,), (0,)), ((), ())), preferred_element_type=F32)


def _resident(shape):
    nd = len(shape)
    return pl.BlockSpec(shape, lambda *_: (0,) * nd, pipeline_mode=pl.Buffered(1))


def _layer_norm_rows(z, g, b):
    mu = jnp.mean(z, axis=-1, keepdims=True)
    zc = z - mu
    var = jnp.mean(zc * zc, axis=-1, keepdims=True)
    return zc * lax.rsqrt(var + LN_EPS) * g + b


def _inproj_kernel(x_ref, wqk_ref, bqk_ref, cw_ref, wvo_ref, bvo_ref, wif_ref, bif_ref,
                   wb_ref, bb_ref, wg_ref, bg_ref,
                   qk_ref, vo_ref, if_ref, qkvb_ref, gg_ref, *ubuf_refs, tm, tiles_per_seq):
    i = pl.program_id(0)
    xb = x_ref[...].astype(BF16)
    halo = MLSTM_CONV - 1

    @pl.when(i == 0)
    def _():
        for ubuf_ref in ubuf_refs:
            ubuf_ref[tm:tm + SUBLANES, :] = jnp.zeros((SUBLANES, PROJ_CHUNK), F32)

    at_start = jnp.full((SUBLANES, PROJ_CHUNK), i % tiles_per_seq, jnp.int32) == 0

    def conv_chunk(c):
        ubuf_ref = ubuf_refs[c]
        cs = slice(c * PROJ_CHUNK, (c + 1) * PROJ_CHUNK)
        u = _dot(xb, wqk_ref[:, cs]) + bqk_ref[:, cs]
        ubuf_ref[0:SUBLANES, :] = jnp.where(at_start, 0.0, ubuf_ref[tm:tm + SUBLANES, :])
        ubuf_ref[SUBLANES:SUBLANES + tm, :] = u
        conv = cw_ref[halo:halo + 1, cs] * u
        for k in range(halo):
            off = SUBLANES - halo + k
            conv = conv + cw_ref[k:k + 1, cs] * ubuf_ref[off:off + tm, :]
        act = conv * jax.nn.sigmoid(conv)
        if c * PROJ_CHUNK >= MLSTM_WIDTH:
            act = act * (MLSTM_HEAD_DIM ** -0.5)
        qk_ref[:, cs] = act.astype(qk_ref.dtype)

    def plain_chunk(w_ref, b_ref, o_ref, c):
        cs = slice(c * PROJ_CHUNK, (c + 1) * PROJ_CHUNK)
        o_ref[:, cs] = (_dot(xb, w_ref[:, cs]) + b_ref[:, cs]).astype(o_ref.dtype)

    plain = ([(wvo_ref, bvo_ref, vo_ref, c) for c in range(2 * MLSTM_WIDTH // PROJ_CHUNK)]
             + [(wb_ref, bb_ref, qkvb_ref, c) for c in range(3 * MOBA_WIDTH // PROJ_CHUNK)]
             + [(wg_ref, bg_ref, gg_ref, c) for c in range(2 * D_MODEL // PROJ_CHUNK)])
    per_conv = -(-len(plain) // len(ubuf_refs))
    for c in range(len(ubuf_refs)):
        conv_chunk(c)
        for job in plain[c * per_conv:(c + 1) * per_conv]:
            plain_chunk(*job)
    if_ref[...] = _dot(xb, wif_ref[...]) + bif_ref[...]


def _in_proj(x2d, w_in, b_in, conv_qk, seq_len, tm=512):
    m_rows = x2d.shape[0]
    w4 = 4 * MLSTM_WIDTH
    nh = MLSTM_HEADS
    b0 = w4 + 2 * nh
    scale_b = MOBA_HEAD_DIM ** -0.5 * LOG2E
    col_scale = jnp.concatenate([jnp.full((MOBA_WIDTH,), scale_b, F32), jnp.ones((2 * MOBA_WIDTH,), F32)])
    wqk = w_in[:, :2 * MLSTM_WIDTH].astype(BF16)
    bqk = b_in[None, :2 * MLSTM_WIDTH]
    wvo = w_in[:, 2 * MLSTM_WIDTH:w4].astype(BF16)
    bvo = b_in[None, 2 * MLSTM_WIDTH:w4]
    wif = jnp.pad(w_in[:, w4:b0], ((0, 0), (0, LANES - 2 * nh))).astype(BF16)
    bif = jnp.pad(b_in[w4:b0], (0, LANES - 2 * nh))[None, :]
    wb = (w_in[:, b0:b0 + 3 * MOBA_WIDTH] * col_scale).astype(BF16)
    bb = (b_in[b0:b0 + 3 * MOBA_WIDTH] * col_scale)[None, :]
    wg = w_in[:, b0 + 3 * MOBA_WIDTH:].astype(BF16)
    bg = b_in[None, b0 + 3 * MOBA_WIDTH:]

    row = lambda n: pl.BlockSpec((tm, n), lambda i: (i, 0))
    n_chunks = 2 * MLSTM_WIDTH // PROJ_CHUNK
    return pl.pallas_call(
        functools.partial(_inproj_kernel, tm=tm, tiles_per_seq=seq_len // tm),
        grid=(m_rows // tm,),
        in_specs=[row(D_MODEL),
                  _resident(wqk.shape), _resident(bqk.shape), _resident(conv_qk.shape),
                  _resident(wvo.shape), _resident(bvo.shape),
                  _resident(wif.shape), _resident(bif.shape),
                  _resident(wb.shape), _resident(bb.shape),
                  _resident(wg.shape), _resident(bg.shape)],
        out_specs=[row(2 * MLSTM_WIDTH), row(2 * MLSTM_WIDTH), row(LANES), row(3 * MOBA_WIDTH), row(2 * D_MODEL)],
        out_shape=[jax.ShapeDtypeStruct((m_rows, 2 * MLSTM_WIDTH), BF16),
                   jax.ShapeDtypeStruct((m_rows, 2 * MLSTM_WIDTH), BF16),
                   jax.ShapeDtypeStruct((m_rows, LANES), F32),
                   jax.ShapeDtypeStruct((m_rows, 3 * MOBA_WIDTH), BF16),
                   jax.ShapeDtypeStruct((m_rows, 2 * D_MODEL), BF16)],
        scratch_shapes=[pltpu.VMEM((tm + SUBLANES, PROJ_CHUNK), F32)] * n_chunks,
        compiler_params=pltpu.CompilerParams(dimension_semantics=("arbitrary",),
                                             vmem_limit_bytes=VMEM_LIMIT_BYTES),
        name="in_proj",
    )(x2d, wqk, bqk, conv_qk, wvo, bvo, wif, bif, wb, bb, wg, bg)


def _log_sigmoid(x):
    return jnp.minimum(x, 0.0) - jnp.log1p(jnp.exp(-jnp.abs(x)))


def _mlstm_kernel(qk_refs, vo_refs, ift_ref, gain_ref, ya_refs, *state_refs):
    c = pl.program_id(1)
    L = MLSTM_CHUNK
    nh = MLSTM_HEADS
    dh = MLSTM_HEAD_DIM
    n_chains = MLSTM_SEQS_PER_STEP * nh
    c_refs, n_refs, m_refs = state_refs[:n_chains], state_refs[n_chains:2 * n_chains], state_refs[2 * n_chains:]

    @pl.when(c == 0)
    def _():
        for ref in state_refs:
            ref[...] = jnp.zeros_like(ref)

    ri = lax.broadcasted_iota(jnp.int32, (L, L), 0)
    ci = lax.broadcasted_iota(jnp.int32, (L, L), 1)
    eye = ri == ci
    causal = ri >= ci
    lane = lax.broadcasted_iota(jnp.int32, (2 * nh, L), 1)

    gates = []
    for bb in range(MLSTM_SEQS_PER_STEP):
        rows = ift_ref[bb]
        cum = _log_sigmoid(rows)
        shift = 1
        while shift < L:
            cum = cum + jnp.where(lane >= shift, pltpu.roll(cum, shift, axis=1), 0.0)
            shift *= 2
        gates.append((rows, cum))

    for chain in range(n_chains):
        bb, h = divmod(chain, nh)
        rows, cum = gates[bb]
        qk_ref, vo_ref, ya_ref = qk_refs.at[bb], vo_refs.at[bb], ya_refs.at[bb]
        c_ref, n_ref, m_ref = c_refs[chain], n_refs[chain], m_refs[chain]
        hs = slice(h * dh, (h + 1) * dh)
        li_row = rows[h:h + 1, :]
        b_row = cum[nh + h:nh + h + 1, :]
        b_col = jnp.sum(jnp.where(eye, b_row, 0.0), axis=1, keepdims=True)
        li_col = jnp.sum(jnp.where(eye, li_row, 0.0), axis=1, keepdims=True)

        m_prev = m_ref[...]
        d_intra = jnp.where(causal, b_col - b_row + li_row, -jnp.inf)
        inter = b_col + m_prev
        m_q = jnp.maximum(inter, jnp.max(d_intra, axis=1, keepdims=True))
        w_inter = jnp.exp(inter - m_q)

        qb = qk_ref[:, hs]
        kb = qk_ref[:, MLSTM_WIDTH + h * dh:MLSTM_WIDTH + (h + 1) * dh]
        vb = vo_ref[:, hs]
        s = _dot_nt(qb, kb) * jnp.exp(d_intra - m_q)
        num = w_inter * _dot(qb, c_ref[...].astype(BF16)) + _dot(s.astype(BF16), vb)
        qn = jnp.sum(qb.astype(F32) * n_ref[...], axis=1, keepdims=True)
        den = w_inter * qn + jnp.sum(s, axis=1, keepdims=True)
        hh = num / jnp.maximum(jnp.abs(den), jnp.exp(-m_q))

        mu = jnp.mean(hh, axis=1, keepdims=True)
        hc = hh - mu
        var = jnp.mean(hc * hc, axis=1, keepdims=True)
        hn = hc * lax.rsqrt(var + LN_EPS) * gain_ref[:, hs]
        og = vo_ref[:, MLSTM_WIDTH + h * dh:MLSTM_WIDTH + (h + 1) * dh].astype(F32)
        ya_ref[:, hs] = (jax.nn.sigmoid(og) * hn).astype(ya_ref.dtype)

        b_last = b_row[:, L - 1:L]
        d_state_row = b_last - b_row + li_row
        m_new = jnp.maximum(b_last + m_prev, jnp.max(d_state_row, axis=1, keepdims=True))
        w_prev = jnp.exp(b_last + m_prev - m_new)
        w_k = jnp.exp(b_last - b_col + li_col - m_new)
        kw = kb.astype(F32) * w_k
        c_ref[...] = w_prev * c_ref[...] + _dot(kw.T.astype(BF16), vb)
        n_ref[...] = w_prev * n_ref[...] + jnp.sum(kw, axis=0, keepdims=True)
        m_ref[...] = m_new


def _mlstm(qk, vo, ift, gain, batch, seq_len):
    L = MLSTM_CHUNK
    dh = MLSTM_HEAD_DIM
    nh = MLSTM_HEADS
    nc = seq_len // L
    m_rows = qk.shape[0]
    ns = MLSTM_SEQS_PER_STEP
    n_chains = ns * nh
    rows = lambda n: pl.BlockSpec((ns, L, n), lambda b, c: (b, c, 0))
    ya = pl.pallas_call(
        _mlstm_kernel,
        grid=(batch // ns, nc),
        in_specs=[rows(2 * MLSTM_WIDTH), rows(2 * MLSTM_WIDTH),
                  pl.BlockSpec((ns, 2 * nh, L), lambda b, c: (b, 0, c)),
                  pl.BlockSpec((1, MLSTM_WIDTH), lambda b, c: (0, 0))],
        out_specs=rows(MLSTM_WIDTH),
        out_shape=jax.ShapeDtypeStruct((batch, seq_len, MLSTM_WIDTH), BF16),
        scratch_shapes=([pltpu.VMEM((dh, dh), F32)] * n_chains + [pltpu.VMEM((1, dh), F32)] * n_chains
                        + [pltpu.VMEM((1, 1), F32)] * n_chains),
        compiler_params=pltpu.CompilerParams(dimension_semantics=("arbitrary", "arbitrary"),
                                             vmem_limit_bytes=VMEM_LIMIT_BYTES),
        name="mlstm",
    )(qk.reshape(batch, seq_len, 2 * MLSTM_WIDTH), vo.reshape(batch, seq_len, 2 * MLSTM_WIDTH), ift, gain)
    return ya.reshape(m_rows, MLSTM_WIDTH)


def _t5_bucket(rel):
    n = jnp.maximum(rel, 0)
    max_exact = NUM_BUCKETS // 2
    nf = jnp.maximum(n, max_exact).astype(F32)
    large = max_exact + (jnp.log(nf / max_exact) / math.log(REL_MAX_DISTANCE / max_exact)
                         * (NUM_BUCKETS - max_exact)).astype(jnp.int32)
    large = jnp.minimum(large, NUM_BUCKETS - 1)
    return jnp.where(n < max_exact, n, large)


def _moba_kernel(rb_ref, q_ref, k_ref, v_ref, bkt_own_ref, bkt_prev_ref, yb_ref, *scratch, nb):
    hp = MOBA_HEADS_PER_STEP
    per_head = len(scratch) // hp
    step = pl.program_id(2)
    blk = MOBA_BLOCK
    dh = MOBA_HEAD_DIM
    nq = MOBA_QBLOCKS * blk
    G = MOBA_GROUP
    i_lo = MOBA_QBLOCKS * step
    i_hi = i_lo + 1
    lane = lax.broadcasted_iota(jnp.int32, (1, nq), 1)
    n_far = (i_hi + G) // G - 1

    def group_start(t):
        return jnp.maximum(i_hi - G * t - (G - 1), 0)

    class Head:
        def __init__(self, hh):
            (self.kmean_ref, self.bias_ref, self.mask_ref, self.sa_ref, self.sb_ref,
             self.m_ref, self.l_ref, self.acc_ref) = scratch[hh * per_head:(hh + 1) * per_head]
            self.cs = slice(hh * dh, (hh + 1) * dh)
            self.h = hp * pl.program_id(1) + hh
            self.far_bias = rb_ref[NUM_BUCKETS - 1, self.h] * LOG2E

        def prepare(self):
            def mean_body(j, carry):
                kj = k_ref[0, pl.ds(pl.multiple_of(j * blk, blk), blk), self.cs].astype(F32)
                self.kmean_ref[pl.ds(j, 1), :] = jnp.sum(kj, axis=0, keepdims=True) * (1.0 / blk)
                return carry
            lax.fori_loop(0, nb, mean_body, 0)
            bo = bkt_own_ref[...]
            bp = bkt_prev_ref[...]
            own = jnp.full(bo.shape, NEG, F32)
            prev = jnp.zeros(bp.shape, F32)
            for bucket in range(NUM_BUCKETS):
                val = rb_ref[bucket, self.h] * LOG2E
                own = jnp.where(bo == bucket, val, own)
                prev = jnp.where(bp == bucket, val, prev)
            self.bias_ref[0, :, 0:blk] = jnp.full((blk, blk), NEG, F32)
            self.bias_ref[0, :, blk:nq] = own
            self.bias_ref[1, :, 0:blk] = own
            self.bias_ref[1, :, blk:nq] = prev
            self.bias_ref[2, :, 0:blk] = prev
            self.bias_ref[2, :, blk:nq] = jnp.full((blk, blk), self.far_bias, F32)
            self.bias_ref[3] = jnp.full((blk, nq), self.far_bias, F32)

        def choose_blocks(self):
            qb = q_ref[:, self.cs]
            km = self.kmean_ref[...]
            km_hi = km.astype(BF16)
            r1 = km - km_hi.astype(F32)
            km_mid = r1.astype(BF16)
            km_lo = (r1 - km_mid.astype(F32)).astype(BF16)
            gate = _dot_nt(km_hi, qb) + _dot_nt(km_mid, qb) + _dot_nt(km_lo, qb)
            row = lax.broadcasted_iota(jnp.int32, gate.shape, 0)
            own_blk = jnp.where(lane < blk, i_lo, i_hi)
            gate = jnp.where(row < own_blk, gate, -jnp.inf)
            chosen = jnp.zeros(gate.shape, jnp.bool_)
            for r in range(MOBA_TOPK):
                mx = jnp.max(gate, axis=0, keepdims=True)
                first = jnp.min(jnp.where(gate == mx, row, nb), axis=0, keepdims=True)
                pick = row == first + jnp.where(own_blk > r, 0, 2 * nb)
                chosen = jnp.logical_or(chosen, pick)
                gate = jnp.where(pick, -jnp.inf, gate)
            self.mask_ref[0:MASK_PAD, :] = jnp.full((MASK_PAD, nq), NEG, F32)
            self.mask_ref[MASK_PAD:MASK_PAD + nb, :] = jnp.where(chosen, 0.0, NEG)

        def mask_row(self, j, j_top):
            return self.mask_ref[pl.ds(jnp.where(j <= j_top, j + MASK_PAD, 0), 1), :]

        def logits_group(self, t, s_ref):
            qb = q_ref[:, self.cs]
            kg = k_ref[0, pl.ds(pl.multiple_of(group_start(t) * blk, blk), G * blk), self.cs]
            s_ref[...] = _dot_nt(kg, qb)

        def softmax_pv(self, t, s_ref, rows, m_old):
            cmax = [jnp.max(s_ref[r * blk:(r + 1) * blk, :], axis=0, keepdims=True) + rows[r] for r in range(G)]
            m_new = functools.reduce(jnp.maximum, cmax)
            if m_old is not None:
                m_new = jnp.maximum(m_new, m_old)
            lsum = jnp.zeros((1, nq), F32)
            pv = jnp.zeros((dh, nq), F32)
            for r in range(G):
                p = jnp.exp2(s_ref[r * blk:(r + 1) * blk, :] - (m_new - rows[r]))
                lsum = lsum + jnp.sum(p, axis=0, keepdims=True)
                vj = v_ref[0, pl.ds(pl.multiple_of((group_start(t) + r) * blk, blk), blk), self.cs]
                pv = pv + _dot_tn(vj, p.astype(BF16))
            return m_new, lsum, pv

        def first_group(self):
            sa_ref = self.sa_ref
            rows0 = []
            for r in range(G):
                j = group_start(0) + r
                dist = i_hi - j
                sa_ref[r * blk:(r + 1) * blk, :] = (sa_ref[r * blk:(r + 1) * blk, :]
                                                    + self.bias_ref[jnp.clip(dist, 0, 3)])
                own_lanes = jnp.where(dist == 0, nq, jnp.where(dist == 1, blk, 0))
                rows0.append(jnp.where(lane < own_lanes, 0.0, self.mask_row(j, i_hi)))
            m0, l0, pv0 = self.softmax_pv(0, sa_ref, rows0, None)
            self.m_ref[...] = m0
            self.l_ref[...] = l0
            self.acc_ref[...] = pv0

        def far_step(self, t, s_ref, next_ref):
            if next_ref is not None:
                self.logits_group(t + 1, next_ref)
            j_top = i_hi - G * t
            rows = [self.mask_row(group_start(t) + r, j_top) + self.far_bias for r in range(G)]
            m_old = self.m_ref[...]
            m_new, lsum, pv = self.softmax_pv(t, s_ref, rows, m_old)
            alpha = jnp.exp2(m_old - m_new)
            self.l_ref[...] = alpha * self.l_ref[...] + lsum
            self.acc_ref[...] = alpha * self.acc_ref[...] + pv
            self.m_ref[...] = m_new

        def finish(self):
            out_t = self.acc_ref[...] / self.l_ref[...]
            yb_ref[:, self.cs] = out_t.T.astype(yb_ref.dtype)

    heads = [Head(hh) for hh in range(hp)]

    @pl.when(step == 0)
    def _():
        for hd in heads:
            hd.prepare()

    for hd in heads:
        hd.choose_blocks()
        hd.logits_group(0, hd.sa_ref)
        hd.logits_group(1, hd.sb_ref)
    for hd in heads:
        hd.first_group()

    def pair_body(u, carry):
        for hd in heads:
            hd.far_step(2 * u + 1, hd.sb_ref, hd.sa_ref)
        for hd in heads:
            hd.far_step(2 * u + 2, hd.sa_ref, hd.sb_ref)
        return carry
    lax.fori_loop(0, n_far // 2, pair_body, 0)

    @pl.when(n_far % 2 == 1)
    def _():
        for hd in heads:
            hd.far_step(n_far, hd.sb_ref, None)

    for hd in heads:
        hd.finish()


def _moba(qkvb, rel_bias, batch, seq_len):
    blk = MOBA_BLOCK
    dh = MOBA_HEAD_DIM
    hp = MOBA_HEADS_PER_STEP
    ng = MOBA_HEADS // hp
    nb = seq_len // blk
    nq = MOBA_QBLOCKS * blk
    n_steps = nb // MOBA_QBLOCKS
    m_rows = qkvb.shape[0]
    qkvb3 = qkvb.reshape(batch, seq_len, 3 * MOBA_WIDTH)
    pos = jnp.arange(blk, dtype=jnp.int32)
    rel_own = pos[None, :] - pos[:, None]
    bkt_own = jnp.where(rel_own >= 0, _t5_bucket(rel_own), -1)
    bkt_prev = _t5_bucket(rel_own + blk)
    full2 = lambda shape: pl.BlockSpec(shape, lambda b, g, i: (0, 0))
    per_head_scratch = [pltpu.VMEM((nb, dh), F32),
                        pltpu.VMEM((4, blk, nq), F32),
                        pltpu.VMEM((MASK_PAD + nb, nq), F32),
                        pltpu.VMEM((MOBA_GROUP * blk, nq), F32), pltpu.VMEM((MOBA_GROUP * blk, nq), F32),
                        pltpu.VMEM((1, nq), F32), pltpu.VMEM((1, nq), F32), pltpu.VMEM((dh, nq), F32)]
    return pl.pallas_call(
        functools.partial(_moba_kernel, nb=nb),
        grid=(batch, ng, n_steps),
        in_specs=[pl.BlockSpec(memory_space=pltpu.SMEM),
                  pl.BlockSpec((nq, hp * dh), lambda b, g, i: (b * n_steps + i, g)),
                  pl.BlockSpec((1, seq_len, hp * dh), lambda b, g, i: (b, 0, ng + g)),
                  pl.BlockSpec((1, seq_len, hp * dh), lambda b, g, i: (b, 0, 2 * ng + g)),
                  full2((blk, blk)), full2((blk, blk))],
        out_specs=pl.BlockSpec((nq, hp * dh), lambda b, g, i: (b * n_steps + i, g)),
        out_shape=jax.ShapeDtypeStruct((m_rows, MOBA_WIDTH), BF16),
        scratch_shapes=per_head_scratch * hp,
        compiler_params=pltpu.CompilerParams(dimension_semantics=("arbitrary", "arbitrary", "arbitrary"),
                                             vmem_limit_bytes=VMEM_LIMIT_BYTES),
        name="moba",
    )(rel_bias, qkvb, qkvb3, qkvb3, bkt_own, bkt_prev)


def _mixtail_kernel(x_ref, ya_ref, yb_ref, ga_ref, gb_ref, wa_ref, wb_ref, wo_ref, g_ref, b_ref, out_ref):
    pa = _dot(ya_ref[...], wa_ref[...])
    pb = _dot(yb_ref[...], wb_ref[...])
    merged = (jax.nn.sigmoid(ga_ref[...].astype(F32)) * pa
              + jax.nn.sigmoid(gb_ref[...].astype(F32)) * pb)
    mix = _dot(merged.astype(BF16), wo_ref[...])
    z = ALPHA * x_ref[...] + mix
    out_ref[...] = _layer_norm_rows(z, g_ref[...], b_ref[...])


def _mix_tail(x2d, ya, yb, gg, w_a, w_b, w_o, ln_g, ln_b, tm=512):
    m_rows = x2d.shape[0]
    row = lambda j: pl.BlockSpec((tm, D_MODEL), lambda i: (i, j))
    wa = w_a.astype(BF16)
    wb = w_b.astype(BF16)
    wo = w_o.astype(BF16)
    return pl.pallas_call(
        _mixtail_kernel,
        grid=(m_rows // tm,),
        in_specs=[row(0), row(0), row(0), row(0), row(1),
                  _resident(wa.shape), _resident(wb.shape), _resident(wo.shape),
                  _resident((1, D_MODEL)), _resident((1, D_MODEL))],
        out_specs=row(0),
        out_shape=jax.ShapeDtypeStruct((m_rows, D_MODEL), F32),
        compiler_params=pltpu.CompilerParams(dimension_semantics=("arbitrary",),
                                             vmem_limit_bytes=VMEM_LIMIT_BYTES),
        name="mix_tail",
    )(x2d, ya, yb, gg, gg, wa, wb, wo, ln_g[None, :], ln_b[None, :])


def _ffn_kernel(x_ref, wup_ref, cw_ref, wdn_ref, g_ref, b_ref, out_ref,
                xb_ref, ua_ref, ub_ref, carry_ref, acc_ref, *, tm, tiles_per_seq):
    i = pl.program_id(0)
    halo = FFN_CONV - 1
    xb_ref[...] = x_ref[...].astype(BF16)

    @pl.when(i == 0)
    def _():
        carry_ref[...] = jnp.zeros_like(carry_ref)

    at_start = jnp.full((SUBLANES, 2 * FFN_CHUNK), i % tiles_per_seq, jnp.int32) == 0

    def up_proj(c, u_ref):
        u = _dot(xb_ref[...], wup_ref[c])
        u_ref[0:SUBLANES, :] = jnp.where(at_start, 0.0, carry_ref[c])
        u_ref[SUBLANES:SUBLANES + tm, :] = u
        carry_ref[c] = u[tm - SUBLANES:tm, :]

    def act_down(c, u_ref):
        cw = cw_ref[c]
        conv = cw[halo:halo + 1, :] * u_ref[SUBLANES:SUBLANES + tm, :]
        for k in range(halo):
            off = SUBLANES - halo + k
            conv = conv + cw[k:k + 1, :] * u_ref[off:off + tm, :]
        gate = conv[:, :FFN_CHUNK]
        up = conv[:, FFN_CHUNK:]
        act = (gate * jax.nn.sigmoid(gate) * up).astype(BF16)
        if c == 0:
            acc_ref[...] = _dot(act, wdn_ref[c])
        else:
            acc_ref[...] += _dot(act, wdn_ref[c])

    up_proj(0, ua_ref)
    for p in range(N_FFN_CHUNKS // 2):
        up_proj(2 * p + 1, ub_ref)
        act_down(2 * p, ua_ref)
        up_proj(2 * p + 2, ua_ref)
        act_down(2 * p + 1, ub_ref)
    act_down(N_FFN_CHUNKS - 1, ua_ref)

    z = ALPHA * x_ref[...] + acc_ref[...]
    out_ref[...] = _layer_norm_rows(z, g_ref[...], b_ref[...])


def _ffn(x2d, w_up, conv_ffn, w_down, ln_g, ln_b, seq_len, tm=512):
    m_rows = x2d.shape[0]
    nch = N_FFN_CHUNKS
    fc = FFN_CHUNK

    def pair(t):
        lead = t.shape[:-1]
        t = t.reshape(*lead, 2, nch, fc)
        t = jnp.moveaxis(t, -2, 0)
        return t.reshape(nch, *lead, 2 * fc)

    wup = pair(w_up).astype(BF16)
    cw = pair(conv_ffn)
    wdn = w_down.reshape(nch, fc, D_MODEL).astype(BF16)
    row = pl.BlockSpec((tm, D_MODEL), lambda i: (i, 0))
    return pl.pallas_call(
        functools.partial(_ffn_kernel, tm=tm, tiles_per_seq=seq_len // tm),
        grid=(m_rows // tm,),
        in_specs=[row, _resident(wup.shape), _resident(cw.shape), _resident(wdn.shape),
                  _resident((1, D_MODEL)), _resident((1, D_MODEL))],
        out_specs=row,
        out_shape=jax.ShapeDtypeStruct((m_rows, D_MODEL), F32),
        scratch_shapes=[pltpu.VMEM((tm, D_MODEL), BF16),
                        pltpu.VMEM((tm + SUBLANES, 2 * fc), F32), pltpu.VMEM((tm + SUBLANES, 2 * fc), F32),
                        pltpu.VMEM((nch, SUBLANES, 2 * fc), F32),
                        pltpu.VMEM((tm, D_MODEL), F32)],
        compiler_params=pltpu.CompilerParams(dimension_semantics=("arbitrary",),
                                             vmem_limit_bytes=VMEM_LIMIT_BYTES),
        name="ffn",
    )(x2d, wup, cw, wdn, ln_g[None, :], ln_b[None, :])


def kernel(x, w_in, b_in, conv_qk, mlstm_norm, rel_bias, w_branch_a, w_branch_b, w_out,
           ln1_g, ln1_b, w_up, conv_ffn, w_down, ln2_g, ln2_b):
    batch, seq_len, d_model = x.shape
    assert d_model == D_MODEL
    assert seq_len % (MOBA_GROUP * MOBA_BLOCK) == 0
    assert batch % MLSTM_SEQS_PER_STEP == 0
    h = x.reshape(batch * seq_len, d_model)
    for l in range(DEPTH):
        qk, vo, ifp, qkvb, gg = _in_proj(h, w_in[l], b_in[l], conv_qk[l], seq_len)
        ift = ifp[:, :2 * MLSTM_HEADS].reshape(batch, seq_len, 2 * MLSTM_HEADS).transpose(0, 2, 1)
        ya = _mlstm(qk, vo, ift, mlstm_norm[l][None, :], batch, seq_len)
        yb = _moba(qkvb, rel_bias, batch, seq_len)
        h = _mix_tail(h, ya, yb, gg, w_branch_a[l], w_branch_b[l], w_out[l], ln1_g[l], ln1_b[l])
        h = _ffn(h, w_up[l], conv_ffn[l], w_down[l], ln2_g[l], ln2_b[l], seq_len)
    return h.reshape(batch, seq_len, d_model)
```

```python
import functools
import math

import jax
import jax.numpy as jnp
from jax import lax
from jax.experimental import pallas as pl
from jax.experimental.pallas import tpu as pltpu

D_MODEL = 1024
DEPTH = 2
MLSTM_HEADS = 4
MLSTM_HEAD_DIM = 256
MLSTM_WIDTH = MLSTM_HEADS * MLSTM_HEAD_DIM
MLSTM_CONV = 4
MOBA_HEADS = 8
MOBA_HEAD_DIM = 128
MOBA_WIDTH = MOBA_HEADS * MOBA_HEAD_DIM
MOBA_BLOCK = 256
MOBA_TOPK = 3
NUM_BUCKETS = 32
REL_MAX_DISTANCE = 128
D_FF = 2816
FFN_CONV = 3
ALPHA = (2 * DEPTH) ** 0.25
LN_EPS = 1e-5

SUBLANES = 8
LANES = 128
VMEM_LIMIT_BYTES = 56 * 1024 * 1024

MLSTM_CHUNK = 256
MLSTM_SEQS_PER_STEP = 1
FFN_CHUNK = 256
N_FFN_CHUNKS = D_FF // FFN_CHUNK
PROJ_CHUNK = 512
NEG = -1e30
MOBA_GROUP = 4
MOBA_QBLOCKS = 2
MOBA_HEADS_PER_STEP = 2
LOG2E = math.log2(math.e)
MASK_PAD = SUBLANES

BF16 = jnp.bfloat16
F32 = jnp.float32


def _dot(a, b):
    return jnp.dot(a, b, preferred_element_type=F32)


def _dot_nt(a, b):
    return lax.dot_general(a, b, (((1,), (1,)), ((), ())), preferred_element_type=F32)


def _dot_tn(a, b):
    return lax.dot_general(a, b, (((0,), (0,)), ((), ())), preferred_element_type=F32)


def _resident(shape):
    nd = len(shape)
    return pl.BlockSpec(shape, lambda *_: (0,) * nd, pipeline_mode=pl.Buffered(1))


def _layer_norm_rows(z, g, b):
    mu = jnp.mean(z, axis=-1, keepdims=True)
    zc = z - mu
    var = jnp.mean(zc * zc, axis=-1, keepdims=True)
    return zc * lax.rsqrt(var + LN_EPS) * g + b


def _inproj_kernel(x_ref, wqk_ref, bqk_ref, cw_ref, wvo_ref, bvo_ref, wif_ref, bif_ref,
                   wb_ref, bb_ref, wg_ref, bg_ref,
                   qk_ref, vo_ref, if_ref, qkvb_ref, gg_ref, *ubuf_refs, tm, tiles_per_seq):
    i = pl.program_id(0)
    xb = x_ref[...].astype(BF16)
    halo = MLSTM_CONV - 1

    @pl.when(i == 0)
    def _():
        for ubuf_ref in ubuf_refs:
            ubuf_ref[tm:tm + SUBLANES, :] = jnp.zeros((SUBLANES, PROJ_CHUNK), F32)

    at_start = jnp.full((SUBLANES, PROJ_CHUNK), i % tiles_per_seq, jnp.int32) == 0

    def conv_chunk(c):
        ubuf_ref = ubuf_refs[c]
        cs = slice(c * PROJ_CHUNK, (c + 1) * PROJ_CHUNK)
        u = _dot(xb, wqk_ref[:, cs]) + bqk_ref[:, cs]
        ubuf_ref[0:SUBLANES, :] = jnp.where(at_start, 0.0, ubuf_ref[tm:tm + SUBLANES, :])
        ubuf_ref[SUBLANES:SUBLANES + tm, :] = u
        conv = cw_ref[halo:halo + 1, cs] * u
        for k in range(halo):
            off = SUBLANES - halo + k
            conv = conv + cw_ref[k:k + 1, cs] * ubuf_ref[off:off + tm, :]
        act = conv * jax.nn.sigmoid(conv)
        if c * PROJ_CHUNK >= MLSTM_WIDTH:
            act = act * (MLSTM_HEAD_DIM ** -0.5)
        qk_ref[:, cs] = act.astype(qk_ref.dtype)

    def plain_chunk(w_ref, b_ref, o_ref, c):
        cs = slice(c * PROJ_CHUNK, (c + 1) * PROJ_CHUNK)
        o_ref[:, cs] = (_dot(xb, w_ref[:, cs]) + b_ref[:, cs]).astype(o_ref.dtype)

    plain = ([(wvo_ref, bvo_ref, vo_ref, c) for c in range(2 * MLSTM_WIDTH // PROJ_CHUNK)]
             + [(wb_ref, bb_ref, qkvb_ref, c) for c in range(3 * MOBA_WIDTH // PROJ_CHUNK)]
             + [(wg_ref, bg_ref, gg_ref, c) for c in range(2 * D_MODEL // PROJ_CHUNK)])
    per_conv = -(-len(plain) // len(ubuf_refs))
    for c in range(len(ubuf_refs)):
        conv_chunk(c)
        for job in plain[c * per_conv:(c + 1) * per_conv]:
            plain_chunk(*job)
    if_ref[...] = _dot(xb, wif_ref[...]) + bif_ref[...]


def _in_proj(x2d, w_in, b_in, conv_qk, seq_len, tm=512):
    m_rows = x2d.shape[0]
    w4 = 4 * MLSTM_WIDTH
    nh = MLSTM_HEADS
    b0 = w4 + 2 * nh
    scale_b = MOBA_HEAD_DIM ** -0.5 * LOG2E
    col_scale = jnp.concatenate([jnp.full((MOBA_WIDTH,), scale_b, F32), jnp.ones((2 * MOBA_WIDTH,), F32)])
    wqk = w_in[:, :2 * MLSTM_WIDTH].astype(BF16)
    bqk = b_in[None, :2 * MLSTM_WIDTH]
    wvo = w_in[:, 2 * MLSTM_WIDTH:w4].astype(BF16)
    bvo = b_in[None, 2 * MLSTM_WIDTH:w4]
    wif = jnp.pad(w_in[:, w4:b0], ((0, 0), (0, LANES - 2 * nh))).astype(BF16)
    bif = jnp.pad(b_in[w4:b0], (0, LANES - 2 * nh))[None, :]
    wb = (w_in[:, b0:b0 + 3 * MOBA_WIDTH] * col_scale).astype(BF16)
    bb = (b_in[b0:b0 + 3 * MOBA_WIDTH] * col_scale)[None, :]
    wg = w_in[:, b0 + 3 * MOBA_WIDTH:].astype(BF16)
    bg = b_in[None, b0 + 3 * MOBA_WIDTH:]

    row = lambda n: pl.BlockSpec((tm, n), lambda i: (i, 0))
    n_chunks = 2 * MLSTM_WIDTH // PROJ_CHUNK
    return pl.pallas_call(
        functools.partial(_inproj_kernel, tm=tm, tiles_per_seq=seq_len // tm),
        grid=(m_rows // tm,),
        in_specs=[row(D_MODEL),
                  _resident(wqk.shape), _resident(bqk.shape), _resident(conv_qk.shape),
                  _resident(wvo.shape), _resident(bvo.shape),
                  _resident(wif.shape), _resident(bif.shape),
                  _resident(wb.shape), _resident(bb.shape),
                  _resident(wg.shape), _resident(bg.shape)],
        out_specs=[row(2 * MLSTM_WIDTH), row(2 * MLSTM_WIDTH), row(LANES), row(3 * MOBA_WIDTH), row(2 * D_MODEL)],
        out_shape=[jax.ShapeDtypeStruct((m_rows, 2 * MLSTM_WIDTH), BF16),
                   jax.ShapeDtypeStruct((m_rows, 2 * MLSTM_WIDTH), BF16),
                   jax.ShapeDtypeStruct((m_rows, LANES), F32),
                   jax.ShapeDtypeStruct((m_rows, 3 * MOBA_WIDTH), BF16),
                   jax.ShapeDtypeStruct((m_rows, 2 * D_MODEL), BF16)],
        scratch_shapes=[pltpu.VMEM((tm + SUBLANES, PROJ_CHUNK), F32)] * n_chunks,
        compiler_params=pltpu.CompilerParams(dimension_semantics=("arbitrary",),
                                             vmem_limit_bytes=VMEM_LIMIT_BYTES),
        name="in_proj",
    )(x2d, wqk, bqk, conv_qk, wvo, bvo, wif, bif, wb, bb, wg, bg)


def _log_sigmoid(x):
    return jnp.minimum(x, 0.0) - jnp.log1p(jnp.exp(-jnp.abs(x)))


def _mlstm_kernel(qk_refs, vo_refs, ift_ref, gain_ref, ya_refs, *state_refs):
    c = pl.program_id(1)
    L = MLSTM_CHUNK
    nh = MLSTM_HEADS
    dh = MLSTM_HEAD_DIM
    n_chains = MLSTM_SEQS_PER_STEP * nh
    c_refs, n_refs, m_refs = state_refs[:n_chains], state_refs[n_chains:2 * n_chains], state_refs[2 * n_chains:]

    @pl.when(c == 0)
    def _():
        for ref in state_refs:
            ref[...] = jnp.zeros_like(ref)

    ri = lax.broadcasted_iota(jnp.int32, (L, L), 0)
    ci = lax.broadcasted_iota(jnp.int32, (L, L), 1)
    eye = ri == ci
    causal = ri >= ci
    lane = lax.broadcasted_iota(jnp.int32, (2 * nh, L), 1)

    gates = []
    for bb in range(MLSTM_SEQS_PER_STEP):
        rows = ift_ref[bb]
        cum = _log_sigmoid(rows)
        shift = 1
        while shift < L:
            cum = cum + jnp.where(lane >= shift, pltpu.roll(cum, shift, axis=1), 0.0)
            shift *= 2
        gates.append((rows, cum))

    for chain in range(n_chains):
        bb, h = divmod(chain, nh)
        rows, cum = gates[bb]
        qk_ref, vo_ref, ya_ref = qk_refs.at[bb], vo_refs.at[bb], ya_refs.at[bb]
        c_ref, n_ref, m_ref = c_refs[chain], n_refs[chain], m_refs[chain]
        hs = slice(h * dh, (h + 1) * dh)
        li_row = rows[h:h + 1, :]
        b_row = cum[nh + h:nh + h + 1, :]
        b_col = jnp.sum(jnp.where(eye, b_row, 0.0), axis=1, keepdims=True)
        li_col = jnp.sum(jnp.where(eye, li_row, 0.0), axis=1, keepdims=True)

        m_prev = m_ref[...]
        d_intra = jnp.where(causal, b_col - b_row + li_row, -jnp.inf)
        inter = b_col + m_prev
        m_q = jnp.maximum(inter, jnp.max(d_intra, axis=1, keepdims=True))
        w_inter = jnp.exp(inter - m_q)

        qb = qk_ref[:, hs]
        kb = qk_ref[:, MLSTM_WIDTH + h * dh:MLSTM_WIDTH + (h + 1) * dh]
        vb = vo_ref[:, hs]
        s = _dot_nt(qb, kb) * jnp.exp(d_intra - m_q)
        num = w_inter * _dot(qb, c_ref[...].astype(BF16)) + _dot(s.astype(BF16), vb)
        qn = jnp.sum(qb.astype(F32) * n_ref[...], axis=1, keepdims=True)
        den = w_inter * qn + jnp.sum(s, axis=1, keepdims=True)
        hh = num / jnp.maximum(jnp.abs(den), jnp.exp(-m_q))

        mu = jnp.mean(hh, axis=1, keepdims=True)
        hc = hh - mu
        var = jnp.mean(hc * hc, axis=1, keepdims=True)
        hn = hc * lax.rsqrt(var + LN_EPS) * gain_ref[:, hs]
        og = vo_ref[:, MLSTM_WIDTH + h * dh:MLSTM_WIDTH + (h + 1) * dh].astype(F32)
        ya_ref[:, hs] = (jax.nn.sigmoid(og) * hn).astype(ya_ref.dtype)

        b_last = b_row[:, L - 1:L]
        d_state_row = b_last - b_row + li_row
        m_new = jnp.maximum(b_last + m_prev, jnp.max(d_state_row, axis=1, keepdims=True))
        w_prev = jnp.exp(b_last + m_prev - m_new)
        w_k = jnp.exp(b_last - b_col + li_col - m_new)
        kw = kb.astype(F32) * w_k
        c_ref[...] = w_prev * c_ref[...] + _dot(kw.T.astype(BF16), vb)
        n_ref[...] = w_prev * n_ref[...] + jnp.sum(kw, axis=0, keepdims=True)
        m_ref[...] = m_new


def _mlstm(qk, vo, ift, gain, batch, seq_len):
    L = MLSTM_CHUNK
    dh = MLSTM_HEAD_DIM
    nh = MLSTM_HEADS
    nc = seq_len // L
    m_rows = qk.shape[0]
    ns = MLSTM_SEQS_PER_STEP
    n_chains = ns * nh
    rows = lambda n: pl.BlockSpec((ns, L, n), lambda b, c: (b, c, 0))
    ya = pl.pallas_call(
        _mlstm_kernel,
        grid=(batch // ns, nc),
        in_specs=[rows(2 * MLSTM_WIDTH), rows(2 * MLSTM_WIDTH),
                  pl.BlockSpec((ns, 2 * nh, L), lambda b, c: (b, 0, c)),
                  pl.BlockSpec((1, MLSTM_WIDTH), lambda b, c: (0, 0))],
        out_specs=rows(MLSTM_WIDTH),
        out_shape=jax.ShapeDtypeStruct((batch, seq_len, MLSTM_WIDTH), BF16),
        scratch_shapes=([pltpu.VMEM((dh, dh), F32)] * n_chains + [pltpu.VMEM((1, dh), F32)] * n_chains
                        + [pltpu.VMEM((1, 1), F32)] * n_chains),
        compiler_params=pltpu.CompilerParams(dimension_semantics=("arbitrary", "arbitrary"),
                                             vmem_limit_bytes=VMEM_LIMIT_BYTES),
        name="mlstm",
    )(qk.reshape(batch, seq_len, 2 * MLSTM_WIDTH), vo.reshape(batch, seq_len, 2 * MLSTM_WIDTH), ift, gain)
    return ya.reshape(m_rows, MLSTM_WIDTH)


def _t5_bucket(rel):
    n = jnp.maximum(rel, 0)
    max_exact = NUM_BUCKETS // 2
    nf = jnp.maximum(n, max_exact).astype(F32)
    large = max_exact + (jnp.log(nf / max_exact) / math.log(REL_MAX_DISTANCE / max_exact)
                         * (NUM_BUCKETS - max_exact)).astype(jnp.int32)
    large = jnp.minimum(large, NUM_BUCKETS - 1)
    return jnp.where(n < max_exact, n, large)


def _moba_kernel(rb_ref, q_ref, k_ref, v_ref, bias_refs, member_ref, yb_ref, *scratch, nb):
    hp = MOBA_HEADS_PER_STEP
    per_head = len(scratch) // hp
    step = pl.program_id(2)
    blk = MOBA_BLOCK
    dh = MOBA_HEAD_DIM
    nq = MOBA_QBLOCKS * blk
    G = MOBA_GROUP
    i_lo = MOBA_QBLOCKS * step
    i_hi = i_lo + 1
    lane = lax.broadcasted_iota(jnp.int32, (1, nq), 1)
    n_far = (i_hi + G) // G - 1

    def group_start(t):
        return jnp.maximum(i_hi - G * t - (G - 1), 0)

    class Head:
        def __init__(self, hh):
            (self.kmean_ref, self.mask_ref, self.sa_ref, self.sb_ref,
             self.m_ref, self.l_ref, self.acc_ref) = scratch[hh * per_head:(hh + 1) * per_head]
            self.bias_ref = bias_refs.at[hh]
            self.cs = slice(hh * dh, (hh + 1) * dh)
            self.h = hp * pl.program_id(1) + hh
            self.far_bias = rb_ref[NUM_BUCKETS - 1, self.h] * LOG2E

        def prepare(self):
            self.kmean_ref[...] = _dot(member_ref[...], k_ref[0, :, self.cs]) * (1.0 / blk)

        def choose_blocks(self):
            qb = q_ref[:, self.cs]
            km = self.kmean_ref[...]
            km_hi = km.astype(BF16)
            r1 = km - km_hi.astype(F32)
            km_mid = r1.astype(BF16)
            km_lo = (r1 - km_mid.astype(F32)).astype(BF16)
            gate = _dot_nt(km_hi, qb) + _dot_nt(km_mid, qb) + _dot_nt(km_lo, qb)
            row = lax.broadcasted_iota(jnp.int32, gate.shape, 0)
            own_blk = jnp.where(lane < blk, i_lo, i_hi)
            gate = jnp.where(row < own_blk, gate, -jnp.inf)
            chosen = jnp.zeros(gate.shape, jnp.bool_)
            for r in range(MOBA_TOPK):
                mx = jnp.max(gate, axis=0, keepdims=True)
                first = jnp.min(jnp.where(gate == mx, row, nb), axis=0, keepdims=True)
                pick = row == first + jnp.where(own_blk > r, 0, 2 * nb)
                chosen = jnp.logical_or(chosen, pick)
                gate = jnp.where(pick, -jnp.inf, gate)
            self.mask_ref[0:MASK_PAD, :] = jnp.full((MASK_PAD, nq), NEG, F32)
            self.mask_ref[MASK_PAD:MASK_PAD + nb, :] = jnp.where(chosen, 0.0, NEG)

        def mask_row(self, j, j_top):
            return self.mask_ref[pl.ds(jnp.where(j <= j_top, j + MASK_PAD, 0), 1), :]

        def logits_group(self, t, s_ref):
            qb = q_ref[:, self.cs]
            kg = k_ref[0, pl.ds(pl.multiple_of(group_start(t) * blk, blk), G * blk), self.cs]
            s_ref[...] = _dot_nt(kg, qb)

        def softmax_pv(self, t, s_ref, rows, m_old):
            cmax = [jnp.max(s_ref[r * blk:(r + 1) * blk, :], axis=0, keepdims=True) + rows[r] for r in range(G)]
            m_new = functools.reduce(jnp.maximum, cmax)
            if m_old is not None:
                m_new = jnp.maximum(m_new, m_old)
            lsum = jnp.zeros((1, nq), F32)
            pv = jnp.zeros((dh, nq), F32)
            for r in range(G):
                p = jnp.exp2(s_ref[r * blk:(r + 1) * blk, :] - (m_new - rows[r]))
                lsum = lsum + jnp.sum(p, axis=0, keepdims=True)
                vj = v_ref[0, pl.ds(pl.multiple_of((group_start(t) + r) * blk, blk), blk), self.cs]
                pv = pv + _dot_tn(vj, p.astype(BF16))
            return m_new, lsum, pv

        def first_group(self):
            sa_ref = self.sa_ref
            rows0 = []
            for r in range(G):
                j = group_start(0) + r
                dist = i_hi - j
                sa_ref[r * blk:(r + 1) * blk, :] = (sa_ref[r * blk:(r + 1) * blk, :]
                                                    + self.bias_ref[jnp.clip(dist, 0, 3)])
                own_lanes = jnp.where(dist == 0, nq, jnp.where(dist == 1, blk, 0))
                rows0.append(jnp.where(lane < own_lanes, 0.0, self.mask_row(j, i_hi)))
            m0, l0, pv0 = self.softmax_pv(0, sa_ref, rows0, None)
            self.m_ref[...] = m0
            self.l_ref[...] = l0
            self.acc_ref[...] = pv0

        def far_step(self, t, s_ref, next_ref):
            if next_ref is not None:
                self.logits_group(t + 1, next_ref)
            j_top = i_hi - G * t
            rows = [self.mask_row(group_start(t) + r, j_top) + self.far_bias for r in range(G)]
            m_old = self.m_ref[...]
            m_new, lsum, pv = self.softmax_pv(t, s_ref, rows, m_old)
            alpha = jnp.exp2(m_old - m_new)
            self.l_ref[...] = alpha * self.l_ref[...] + lsum
            self.acc_ref[...] = alpha * self.acc_ref[...] + pv
            self.m_ref[...] = m_new

        def finish(self):
            out_t = self.acc_ref[...] / self.l_ref[...]
            yb_ref[:, self.cs] = out_t.T.astype(yb_ref.dtype)

    heads = [Head(hh) for hh in range(hp)]

    @pl.when(step == 0)
    def _():
        for hd in heads:
            hd.prepare()

    for hd in heads:
        hd.choose_blocks()
        hd.logits_group(0, hd.sa_ref)
        hd.logits_group(1, hd.sb_ref)
    for hd in heads:
        hd.first_group()

    def pair_body(u, carry):
        for hd in heads:
            hd.far_step(2 * u + 1, hd.sb_ref, hd.sa_ref)
        for hd in heads:
            hd.far_step(2 * u + 2, hd.sa_ref, hd.sb_ref)
        return carry
    lax.fori_loop(0, n_far // 2, pair_body, 0)

    @pl.when(n_far % 2 == 1)
    def _():
        for hd in heads:
            hd.far_step(n_far, hd.sb_ref, None)

    for hd in heads:
        hd.finish()


def _moba_bias_kernel(rb_ref, bkt_own_ref, bkt_prev_ref, bias_ref):
    h = pl.program_id(0)
    blk = MOBA_BLOCK
    nq = MOBA_QBLOCKS * blk
    far_bias = rb_ref[NUM_BUCKETS - 1, h] * LOG2E
    bo = bkt_own_ref[...]
    bp = bkt_prev_ref[...]
    own = jnp.full(bo.shape, NEG, F32)
    prev = jnp.zeros(bp.shape, F32)
    for bucket in range(NUM_BUCKETS):
        val = rb_ref[bucket, h] * LOG2E
        own = jnp.where(bo == bucket, val, own)
        prev = jnp.where(bp == bucket, val, prev)
    bias_ref[0, 0, :, 0:blk] = jnp.full((blk, blk), NEG, F32)
    bias_ref[0, 0, :, blk:nq] = own
    bias_ref[0, 1, :, 0:blk] = own
    bias_ref[0, 1, :, blk:nq] = prev
    bias_ref[0, 2, :, 0:blk] = prev
    bias_ref[0, 2, :, blk:nq] = jnp.full((blk, blk), far_bias, F32)
    bias_ref[0, 3] = jnp.full((blk, nq), far_bias, F32)


def _moba_bias(rel_bias):
    blk = MOBA_BLOCK
    nq = MOBA_QBLOCKS * blk
    pos = jnp.arange(blk, dtype=jnp.int32)
    rel_own = pos[None, :] - pos[:, None]
    bkt_own = jnp.where(rel_own >= 0, _t5_bucket(rel_own), -1)
    bkt_prev = _t5_bucket(rel_own + blk)
    full2 = pl.BlockSpec((blk, blk), lambda h: (0, 0))
    return pl.pallas_call(
        _moba_bias_kernel,
        grid=(MOBA_HEADS,),
        in_specs=[pl.BlockSpec(memory_space=pltpu.SMEM), full2, full2],
        out_specs=pl.BlockSpec((1, 4, blk, nq), lambda h: (h, 0, 0, 0)),
        out_shape=jax.ShapeDtypeStruct((MOBA_HEADS, 4, blk, nq), F32),
        compiler_params=pltpu.CompilerParams(dimension_semantics=("arbitrary",)),
        name="moba_bias",
    )(rel_bias, bkt_own, bkt_prev)


def _moba(qkvb, rel_bias, bias, batch, seq_len):
    blk = MOBA_BLOCK
    dh = MOBA_HEAD_DIM
    hp = MOBA_HEADS_PER_STEP
    ng = MOBA_HEADS // hp
    nb = seq_len // blk
    nq = MOBA_QBLOCKS * blk
    n_steps = nb // MOBA_QBLOCKS
    m_rows = qkvb.shape[0]
    qkvb3 = qkvb.reshape(batch, seq_len, 3 * MOBA_WIDTH)
    member = (jnp.arange(seq_len, dtype=jnp.int32)[None, :] // blk
              == jnp.arange(nb, dtype=jnp.int32)[:, None]).astype(BF16)
    per_head_scratch = [pltpu.VMEM((nb, dh), F32),
                        pltpu.VMEM((MASK_PAD + nb, nq), F32),
                        pltpu.VMEM((MOBA_GROUP * blk, nq), F32), pltpu.VMEM((MOBA_GROUP * blk, nq), F32),
                        pltpu.VMEM((1, nq), F32), pltpu.VMEM((1, nq), F32), pltpu.VMEM((dh, nq), F32)]
    return pl.pallas_call(
        functools.partial(_moba_kernel, nb=nb),
        grid=(batch, ng, n_steps),
        in_specs=[pl.BlockSpec(memory_space=pltpu.SMEM),
                  pl.BlockSpec((nq, hp * dh), lambda b, g, i: (b * n_steps + i, g)),
                  pl.BlockSpec((1, seq_len, hp * dh), lambda b, g, i: (b, 0, ng + g)),
                  pl.BlockSpec((1, seq_len, hp * dh), lambda b, g, i: (b, 0, 2 * ng + g)),
                  pl.BlockSpec((hp, 4, blk, nq), lambda b, g, i: (g, 0, 0, 0)),
                  pl.BlockSpec((nb, seq_len), lambda b, g, i: (0, 0))],
        out_specs=pl.BlockSpec((nq, hp * dh), lambda b, g, i: (b * n_steps + i, g)),
        out_shape=jax.ShapeDtypeStruct((m_rows, MOBA_WIDTH), BF16),
        scratch_shapes=per_head_scratch * hp,
        compiler_params=pltpu.CompilerParams(dimension_semantics=("arbitrary", "arbitrary", "arbitrary"),
                                             vmem_limit_bytes=VMEM_LIMIT_BYTES),
        name="moba",
    )(rel_bias, qkvb, qkvb3, qkvb3, bias, member)


def _mixtail_kernel(x_ref, ya_ref, yb_ref, ga_ref, gb_ref, wa_ref, wb_ref, wo_ref, g_ref, b_ref, out_ref):
    pa = _dot(ya_ref[...], wa_ref[...])
    pb = _dot(yb_ref[...], wb_ref[...])
    merged = (jax.nn.sigmoid(ga_ref[...].astype(F32)) * pa
              + jax.nn.sigmoid(gb_ref[...].astype(F32)) * pb)
    mix = _dot(merged.astype(BF16), wo_ref[...])
    z = ALPHA * x_ref[...] + mix
    out_ref[...] = _layer_norm_rows(z, g_ref[...], b_ref[...])


def _mix_tail(x2d, ya, yb, gg, w_a, w_b, w_o, ln_g, ln_b, tm=512):
    m_rows = x2d.shape[0]
    row = lambda j: pl.BlockSpec((tm, D_MODEL), lambda i: (i, j))
    wa = w_a.astype(BF16)
    wb = w_b.astype(BF16)
    wo = w_o.astype(BF16)
    return pl.pallas_call(
        _mixtail_kernel,
        grid=(m_rows // tm,),
        in_specs=[row(0), row(0), row(0), row(0), row(1),
                  _resident(wa.shape), _resident(wb.shape), _resident(wo.shape),
                  _resident((1, D_MODEL)), _resident((1, D_MODEL))],
        out_specs=row(0),
        out_shape=jax.ShapeDtypeStruct((m_rows, D_MODEL), F32),
        compiler_params=pltpu.CompilerParams(dimension_semantics=("arbitrary",),
                                             vmem_limit_bytes=VMEM_LIMIT_BYTES),
        name="mix_tail",
    )(x2d, ya, yb, gg, gg, wa, wb, wo, ln_g[None, :], ln_b[None, :])


def _ffn_kernel(x_ref, wup_ref, cw_ref, wdn_ref, g_ref, b_ref, out_ref,
                xb_ref, ua_ref, ub_ref, carry_ref, acc_ref, *, tm, tiles_per_seq):
    i = pl.program_id(0)
    halo = FFN_CONV - 1
    xb_ref[...] = x_ref[...].astype(BF16)

    @pl.when(i == 0)
    def _():
        carry_ref[...] = jnp.zeros_like(carry_ref)

    at_start = jnp.full((SUBLANES, 2 * FFN_CHUNK), i % tiles_per_seq, jnp.int32) == 0

    def up_proj(c, u_ref):
        u = _dot(xb_ref[...], wup_ref[c])
        u_ref[0:SUBLANES, :] = jnp.where(at_start, 0.0, carry_ref[c])
        u_ref[SUBLANES:SUBLANES + tm, :] = u
        carry_ref[c] = u[tm - SUBLANES:tm, :]

    def act_down(c, u_ref):
        cw = cw_ref[c]
        conv = cw[halo:halo + 1, :] * u_ref[SUBLANES:SUBLANES + tm, :]
        for k in range(halo):
            off = SUBLANES - halo + k
            conv = conv + cw[k:k + 1, :] * u_ref[off:off + tm, :]
        gate = conv[:, :FFN_CHUNK]
        up = conv[:, FFN_CHUNK:]
        act = (gate * jax.nn.sigmoid(gate) * up).astype(BF16)
        if c == 0:
            acc_ref[...] = _dot(act, wdn_ref[c])
        else:
            acc_ref[...] += _dot(act, wdn_ref[c])

    up_proj(0, ua_ref)
    for p in range(N_FFN_CHUNKS // 2):
        up_proj(2 * p + 1, ub_ref)
        act_down(2 * p, ua_ref)
        up_proj(2 * p + 2, ua_ref)
        act_down(2 * p + 1, ub_ref)
    act_down(N_FFN_CHUNKS - 1, ua_ref)

    z = ALPHA * x_ref[...] + acc_ref[...]
    out_ref[...] = _layer_norm_rows(z, g_ref[...], b_ref[...])


def _ffn(x2d, w_up, conv_ffn, w_down, ln_g, ln_b, seq_len, tm=512):
    m_rows = x2d.shape[0]
    nch = N_FFN_CHUNKS
    fc = FFN_CHUNK

    def pair(t):
        lead = t.shape[:-1]
        t = t.reshape(*lead, 2, nch, fc)
        t = jnp.moveaxis(t, -2, 0)
        return t.reshape(nch, *lead, 2 * fc)

    wup = pair(w_up).astype(BF16)
    cw = pair(conv_ffn)
    wdn = w_down.reshape(nch, fc, D_MODEL).astype(BF16)
    row = pl.BlockSpec((tm, D_MODEL), lambda i: (i, 0))
    return pl.pallas_call(
        functools.partial(_ffn_kernel, tm=tm, tiles_per_seq=seq_len // tm),
        grid=(m_rows // tm,),
        in_specs=[row, _resident(wup.shape), _resident(cw.shape), _resident(wdn.shape),
                  _resident((1, D_MODEL)), _resident((1, D_MODEL))],
        out_specs=row,
        out_shape=jax.ShapeDtypeStruct((m_rows, D_MODEL), F32),
        scratch_shapes=[pltpu.VMEM((tm, D_MODEL), BF16),
                        pltpu.VMEM((tm + SUBLANES, 2 * fc), F32), pltpu.VMEM((tm + SUBLANES, 2 * fc), F32),
                        pltpu.VMEM((nch, SUBLANES, 2 * fc), F32),
                        pltpu.VMEM((tm, D_MODEL), F32)],
        compiler_params=pltpu.CompilerParams(dimension_semantics=("arbitrary",),
                                             vmem_limit_bytes=VMEM_LIMIT_BYTES),
        name="ffn",
    )(x2d, wup, cw, wdn, ln_g[None, :], ln_b[None, :])


def kernel(x, w_in, b_in, conv_qk, mlstm_norm, rel_bias, w_branch_a, w_branch_b, w_out,
           ln1_g, ln1_b, w_up, conv_ffn, w_down, ln2_g, ln2_b):
    batch, seq_len, d_model = x.shape
    assert d_model == D_MODEL
    assert seq_len % (MOBA_GROUP * MOBA_BLOCK) == 0
    assert batch % MLSTM_SEQS_PER_STEP == 0
    h = x.reshape(batch * seq_len, d_model)
    moba_bias = _moba_bias(rel_bias)
    for l in range(DEPTH):
        qk, vo, ifp, qkvb, gg = _in_proj(h, w_in[l], b_in[l], conv_qk[l], seq_len)
        ift = ifp[:, :2 * MLSTM_HEADS].reshape(batch, seq_len, 2 * MLSTM_HEADS).transpose(0, 2, 1)
        ya = _mlstm(qk, vo, ift, mlstm_norm[l][None, :], batch, seq_len)
        yb = _moba(qkvb, rel_bias, moba_bias, batch, seq_len)
        h = _mix_tail(h, ya, yb, gg, w_branch_a[l], w_branch_b[l], w_out[l], ln1_g[l], ln1_b[l])
        h = _ffn(h, w_up[l], conv_ffn[l], w_down[l], ln2_g[l], ln2_b[l], seq_len)
    return h.reshape(batch, seq_len, d_model)
```

```python
import functools
import math

import jax
import jax.numpy as jnp
from jax import lax
from jax.experimental import pallas as pl
from jax.experimental.pallas import tpu as pltpu

D_MODEL = 1024
DEPTH = 2
MLSTM_HEADS = 4
MLSTM_HEAD_DIM = 256
MLSTM_WIDTH = MLSTM_HEADS * MLSTM_HEAD_DIM
MLSTM_CONV = 4
MOBA_HEADS = 8
MOBA_HEAD_DIM = 128
MOBA_WIDTH = MOBA_HEADS * MOBA_HEAD_DIM
MOBA_BLOCK = 256
MOBA_TOPK = 3
NUM_BUCKETS = 32
REL_MAX_DISTANCE = 128
D_FF = 2816
FFN_CONV = 3
ALPHA = (2 * DEPTH) ** 0.25
LN_EPS = 1e-5

SUBLANES = 8
LANES = 128
VMEM_LIMIT_BYTES = 56 * 1024 * 1024

MLSTM_CHUNK = 256
MLSTM_SEQS_PER_STEP = 1
FFN_CHUNK = 256
N_FFN_CHUNKS = D_FF // FFN_CHUNK
PROJ_CHUNK = 512
NEG = -1e30
MOBA_GROUP = 4
MOBA_QBLOCKS = 2
MOBA_HEADS_PER_STEP = 2
LOG2E = math.log2(math.e)
MASK_PAD = SUBLANES

BF16 = jnp.bfloat16
F32 = jnp.float32


def _dot(a, b):
    return jnp.dot(a, b, preferred_element_type=F32)


def _dot_nt(a, b):
    return lax.dot_general(a, b, (((1,), (1,)), ((), ())), preferred_element_type=F32)


def _dot_tn(a, b):
    return lax.dot_general(a, b, (((0,), (0,)), ((), ())), preferred_element_type=F32)


def _resident(shape):
    nd = len(shape)
    return pl.BlockSpec(shape, lambda *_: (0,) * nd, pipeline_mode=pl.Buffered(1))


def _layer_norm_rows(z, g, b):
    mu = jnp.mean(z, axis=-1, keepdims=True)
    zc = z - mu
    var = jnp.mean(zc * zc, axis=-1, keepdims=True)
    return zc * lax.rsqrt(var + LN_EPS) * g + b


def _inproj_kernel(x_ref, wqk_ref, bqk_ref, cw_ref, wvo_ref, bvo_ref, wif_ref, bif_ref,
                   wb_ref, bb_ref, wg_ref, bg_ref,
                   qk_ref, vo_ref, ift_ref, qkvb_ref, gg_ref, *ubuf_refs, tm, tiles_per_seq):
    i = pl.program_id(0)
    xb = x_ref[...].astype(BF16)
    halo = MLSTM_CONV - 1

    @pl.when(i == 0)
    def _():
        for ubuf_ref in ubuf_refs:
            ubuf_ref[tm:tm + SUBLANES, :] = jnp.zeros((SUBLANES, PROJ_CHUNK), F32)

    at_start = jnp.full((SUBLANES, PROJ_CHUNK), i % tiles_per_seq, jnp.int32) == 0

    def conv_chunk(c):
        ubuf_ref = ubuf_refs[c]
        cs = slice(c * PROJ_CHUNK, (c + 1) * PROJ_CHUNK)
        u = _dot(xb, wqk_ref[:, cs]) + bqk_ref[:, cs]
        ubuf_ref[0:SUBLANES, :] = jnp.where(at_start, 0.0, ubuf_ref[tm:tm + SUBLANES, :])
        ubuf_ref[SUBLANES:SUBLANES + tm, :] = u
        conv = cw_ref[halo:halo + 1, cs] * u
        for k in range(halo):
            off = SUBLANES - halo + k
            conv = conv + cw_ref[k:k + 1, cs] * ubuf_ref[off:off + tm, :]
        act = conv * jax.nn.sigmoid(conv)
        if c * PROJ_CHUNK >= MLSTM_WIDTH:
            act = act * (MLSTM_HEAD_DIM ** -0.5)
        qk_ref[:, cs] = act.astype(qk_ref.dtype)

    def plain_chunk(w_ref, b_ref, o_ref, c):
        cs = slice(c * PROJ_CHUNK, (c + 1) * PROJ_CHUNK)
        o_ref[:, cs] = (_dot(xb, w_ref[:, cs]) + b_ref[:, cs]).astype(o_ref.dtype)

    plain = ([(wvo_ref, bvo_ref, vo_ref, c) for c in range(2 * MLSTM_WIDTH // PROJ_CHUNK)]
             + [(wb_ref, bb_ref, qkvb_ref, c) for c in range(3 * MOBA_WIDTH // PROJ_CHUNK)]
             + [(wg_ref, bg_ref, gg_ref, c) for c in range(2 * D_MODEL // PROJ_CHUNK)])
    per_conv = -(-len(plain) // len(ubuf_refs))
    for c in range(len(ubuf_refs)):
        conv_chunk(c)
        for job in plain[c * per_conv:(c + 1) * per_conv]:
            plain_chunk(*job)
    gates = _dot(xb, wif_ref[...]) + bif_ref[...]
    ift_ref[0] = gates.T[0:2 * MLSTM_HEADS, :]


def _in_proj(x2d, w_in, b_in, conv_qk, seq_len, tm=512):
    m_rows = x2d.shape[0]
    w4 = 4 * MLSTM_WIDTH
    nh = MLSTM_HEADS
    b0 = w4 + 2 * nh
    scale_b = MOBA_HEAD_DIM ** -0.5 * LOG2E
    col_scale = jnp.concatenate([jnp.full((MOBA_WIDTH,), scale_b, F32), jnp.ones((2 * MOBA_WIDTH,), F32)])
    wqk = w_in[:, :2 * MLSTM_WIDTH].astype(BF16)
    bqk = b_in[None, :2 * MLSTM_WIDTH]
    wvo = w_in[:, 2 * MLSTM_WIDTH:w4].astype(BF16)
    bvo = b_in[None, 2 * MLSTM_WIDTH:w4]
    wif = jnp.pad(w_in[:, w4:b0], ((0, 0), (0, LANES - 2 * nh))).astype(BF16)
    bif = jnp.pad(b_in[w4:b0], (0, LANES - 2 * nh))[None, :]
    wb = (w_in[:, b0:b0 + 3 * MOBA_WIDTH] * col_scale).astype(BF16)
    bb = (b_in[b0:b0 + 3 * MOBA_WIDTH] * col_scale)[None, :]
    wg = w_in[:, b0 + 3 * MOBA_WIDTH:].astype(BF16)
    bg = b_in[None, b0 + 3 * MOBA_WIDTH:]

    row = lambda n: pl.BlockSpec((tm, n), lambda i: (i, 0))
    n_chunks = 2 * MLSTM_WIDTH // PROJ_CHUNK
    tiles_per_seq = seq_len // tm
    return pl.pallas_call(
        functools.partial(_inproj_kernel, tm=tm, tiles_per_seq=tiles_per_seq),
        grid=(m_rows // tm,),
        in_specs=[row(D_MODEL),
                  _resident(wqk.shape), _resident(bqk.shape), _resident(conv_qk.shape),
                  _resident(wvo.shape), _resident(bvo.shape),
                  _resident(wif.shape), _resident(bif.shape),
                  _resident(wb.shape), _resident(bb.shape),
                  _resident(wg.shape), _resident(bg.shape)],
        out_specs=[row(2 * MLSTM_WIDTH), row(2 * MLSTM_WIDTH),
                   pl.BlockSpec((1, 2 * nh, tm), lambda i: (i // tiles_per_seq, 0, i % tiles_per_seq)),
                   row(3 * MOBA_WIDTH), row(2 * D_MODEL)],
        out_shape=[jax.ShapeDtypeStruct((m_rows, 2 * MLSTM_WIDTH), BF16),
                   jax.ShapeDtypeStruct((m_rows, 2 * MLSTM_WIDTH), BF16),
                   jax.ShapeDtypeStruct((m_rows // seq_len, 2 * nh, seq_len), F32),
                   jax.ShapeDtypeStruct((m_rows, 3 * MOBA_WIDTH), BF16),
                   jax.ShapeDtypeStruct((m_rows, 2 * D_MODEL), BF16)],
        scratch_shapes=[pltpu.VMEM((tm + SUBLANES, PROJ_CHUNK), F32)] * n_chunks,
        compiler_params=pltpu.CompilerParams(dimension_semantics=("arbitrary",),
                                             vmem_limit_bytes=VMEM_LIMIT_BYTES),
        name="in_proj",
    )(x2d, wqk, bqk, conv_qk, wvo, bvo, wif, bif, wb, bb, wg, bg)


def _log_sigmoid(x):
    return jnp.minimum(x, 0.0) - jnp.log1p(jnp.exp(-jnp.abs(x)))


def _mlstm_kernel(qk_refs, vo_refs, ift_ref, gain_ref, ya_refs, *state_refs):
    c = pl.program_id(1)
    L = MLSTM_CHUNK
    nh = MLSTM_HEADS
    dh = MLSTM_HEAD_DIM
    n_chains = MLSTM_SEQS_PER_STEP * nh
    c_refs, n_refs, m_refs = state_refs[:n_chains], state_refs[n_chains:2 * n_chains], state_refs[2 * n_chains:]

    @pl.when(c == 0)
    def _():
        for ref in state_refs:
            ref[...] = jnp.zeros_like(ref)

    ri = lax.broadcasted_iota(jnp.int32, (L, L), 0)
    ci = lax.broadcasted_iota(jnp.int32, (L, L), 1)
    eye = ri == ci
    causal = ri >= ci
    lane = lax.broadcasted_iota(jnp.int32, (2 * nh, L), 1)

    gates = []
    for bb in range(MLSTM_SEQS_PER_STEP):
        rows = ift_ref[bb]
        cum = _log_sigmoid(rows)
        shift = 1
        while shift < L:
            cum = cum + jnp.where(lane >= shift, pltpu.roll(cum, shift, axis=1), 0.0)
            shift *= 2
        gates.append((rows, cum))

    for chain in range(n_chains):
        bb, h = divmod(chain, nh)
        rows, cum = gates[bb]
        qk_ref, vo_ref, ya_ref = qk_refs.at[bb], vo_refs.at[bb], ya_refs.at[bb]
        c_ref, n_ref, m_ref = c_refs[chain], n_refs[chain], m_refs[chain]
        hs = slice(h * dh, (h + 1) * dh)
        li_row = rows[h:h + 1, :]
        b_row = cum[nh + h:nh + h + 1, :]
        b_col = jnp.sum(jnp.where(eye, b_row, 0.0), axis=1, keepdims=True)
        li_col = jnp.sum(jnp.where(eye, li_row, 0.0), axis=1, keepdims=True)

        m_prev = m_ref[...]
        d_intra = jnp.where(causal, b_col - b_row + li_row, -jnp.inf)
        inter = b_col + m_prev
        m_q = jnp.maximum(inter, jnp.max(d_intra, axis=1, keepdims=True))
        w_inter = jnp.exp(inter - m_q)

        qb = qk_ref[:, hs]
        kb = qk_ref[:, MLSTM_WIDTH + h * dh:MLSTM_WIDTH + (h + 1) * dh]
        vb = vo_ref[:, hs]
        s = _dot_nt(qb, kb) * jnp.exp(d_intra - m_q)
        num = w_inter * _dot(qb, c_ref[...].astype(BF16)) + _dot(s.astype(BF16), vb)
        qn = jnp.sum(qb.astype(F32) * n_ref[...], axis=1, keepdims=True)
        den = w_inter * qn + jnp.sum(s, axis=1, keepdims=True)
        hh = num / jnp.maximum(jnp.abs(den), jnp.exp(-m_q))

        mu = jnp.mean(hh, axis=1, keepdims=True)
        hc = hh - mu
        var = jnp.mean(hc * hc, axis=1, keepdims=True)
        hn = hc * lax.rsqrt(var + LN_EPS) * gain_ref[:, hs]
        og = vo_ref[:, MLSTM_WIDTH + h * dh:MLSTM_WIDTH + (h + 1) * dh].astype(F32)
        ya_ref[:, hs] = (jax.nn.sigmoid(og) * hn).astype(ya_ref.dtype)

        b_last = b_row[:, L - 1:L]
        d_state_row = b_last - b_row + li_row
        m_new = jnp.maximum(b_last + m_prev, jnp.max(d_state_row, axis=1, keepdims=True))
        w_prev = jnp.exp(b_last + m_prev - m_new)
        w_k = jnp.exp(b_last - b_col + li_col - m_new)
        kw = kb.astype(F32) * w_k
        c_ref[...] = w_prev * c_ref[...] + _dot(kw.T.astype(BF16), vb)
        n_ref[...] = w_prev * n_ref[...] + jnp.sum(kw, axis=0, keepdims=True)
        m_ref[...] = m_new


def _mlstm(qk, vo, ift, gain, batch, seq_len):
    L = MLSTM_CHUNK
    dh = MLSTM_HEAD_DIM
    nh = MLSTM_HEADS
    nc = seq_len // L
    m_rows = qk.shape[0]
    ns = MLSTM_SEQS_PER_STEP
    n_chains = ns * nh
    rows = lambda n: pl.BlockSpec((ns, L, n), lambda b, c: (b, c, 0))
    ya = pl.pallas_call(
        _mlstm_kernel,
        grid=(batch // ns, nc),
        in_specs=[rows(2 * MLSTM_WIDTH), rows(2 * MLSTM_WIDTH),
                  pl.BlockSpec((ns, 2 * nh, L), lambda b, c: (b, 0, c)),
                  pl.BlockSpec((1, MLSTM_WIDTH), lambda b, c: (0, 0))],
        out_specs=rows(MLSTM_WIDTH),
        out_shape=jax.ShapeDtypeStruct((batch, seq_len, MLSTM_WIDTH), BF16),
        scratch_shapes=([pltpu.VMEM((dh, dh), F32)] * n_chains + [pltpu.VMEM((1, dh), F32)] * n_chains
                        + [pltpu.VMEM((1, 1), F32)] * n_chains),
        compiler_params=pltpu.CompilerParams(dimension_semantics=("arbitrary", "arbitrary"),
                                             vmem_limit_bytes=VMEM_LIMIT_BYTES),
        name="mlstm",
    )(qk.reshape(batch, seq_len, 2 * MLSTM_WIDTH), vo.reshape(batch, seq_len, 2 * MLSTM_WIDTH), ift, gain)
    return ya.reshape(m_rows, MLSTM_WIDTH)


def _t5_bucket(rel):
    n = jnp.maximum(rel, 0)
    max_exact = NUM_BUCKETS // 2
    nf = jnp.maximum(n, max_exact).astype(F32)
    large = max_exact + (jnp.log(nf / max_exact) / math.log(REL_MAX_DISTANCE / max_exact)
                         * (NUM_BUCKETS - max_exact)).astype(jnp.int32)
    large = jnp.minimum(large, NUM_BUCKETS - 1)
    return jnp.where(n < max_exact, n, large)


def _moba_kernel(rb_ref, q_ref, k_ref, v_ref, bias_refs, member_ref, yb_ref, *scratch, nb):
    hp = MOBA_HEADS_PER_STEP
    per_head = len(scratch) // hp
    step = pl.program_id(2)
    blk = MOBA_BLOCK
    dh = MOBA_HEAD_DIM
    nq = MOBA_QBLOCKS * blk
    G = MOBA_GROUP
    i_lo = MOBA_QBLOCKS * step
    i_hi = i_lo + 1
    lane = lax.broadcasted_iota(jnp.int32, (1, nq), 1)
    n_far = (i_hi + G) // G - 1

    def group_start(t):
        return jnp.maximum(i_hi - G * t - (G - 1), 0)

    class Head:
        def __init__(self, hh):
            (self.kmean_ref, self.mask_ref, self.sa_ref, self.sb_ref,
             self.m_ref, self.l_ref, self.acc_ref) = scratch[hh * per_head:(hh + 1) * per_head]
            self.bias_ref = bias_refs.at[hh]
            self.cs = slice(hh * dh, (hh + 1) * dh)
            self.h = hp * pl.program_id(1) + hh
            self.far_bias = rb_ref[NUM_BUCKETS - 1, self.h] * LOG2E

        def prepare(self):
            self.kmean_ref[...] = _dot(member_ref[...], k_ref[0, :, self.cs]) * (1.0 / blk)

        def choose_blocks(self):
            qb = q_ref[:, self.cs]
            km = self.kmean_ref[...]
            km_hi = km.astype(BF16)
            r1 = km - km_hi.astype(F32)
            km_mid = r1.astype(BF16)
            km_lo = (r1 - km_mid.astype(F32)).astype(BF16)
            gate = _dot_nt(km_hi, qb) + _dot_nt(km_mid, qb) + _dot_nt(km_lo, qb)
            row = lax.broadcasted_iota(jnp.int32, gate.shape, 0)
            own_blk = jnp.where(lane < blk, i_lo, i_hi)
            gate = jnp.where(row < own_blk, gate, -jnp.inf)
            chosen = jnp.zeros(gate.shape, jnp.bool_)
            for r in range(MOBA_TOPK):
                mx = jnp.max(gate, axis=0, keepdims=True)
                first = jnp.min(jnp.where(gate == mx, row, nb), axis=0, keepdims=True)
                pick = row == first + jnp.where(own_blk > r, 0, 2 * nb)
                chosen = jnp.logical_or(chosen, pick)
                gate = jnp.where(pick, -jnp.inf, gate)
            self.mask_ref[0:MASK_PAD, :] = jnp.full((MASK_PAD, nq), NEG, F32)
            self.mask_ref[MASK_PAD:MASK_PAD + nb, :] = jnp.where(chosen, 0.0, NEG)

        def mask_row(self, j, j_top):
            return self.mask_ref[pl.ds(jnp.where(j <= j_top, j + MASK_PAD, 0), 1), :]

        def logits_group(self, t, s_ref):
            qb = q_ref[:, self.cs]
            kg = k_ref[0, pl.ds(pl.multiple_of(group_start(t) * blk, blk), G * blk), self.cs]
            s_ref[...] = _dot_nt(kg, qb)

        def softmax_pv(self, t, s_ref, rows, m_old):
            cmax = [jnp.max(s_ref[r * blk:(r + 1) * blk, :], axis=0, keepdims=True) + rows[r] for r in range(G)]
            m_new = functools.reduce(jnp.maximum, cmax)
            if m_old is not None:
                m_new = jnp.maximum(m_new, m_old)
            lsum = jnp.zeros((1, nq), F32)
            pv = jnp.zeros((dh, nq), F32)
            for r in range(G):
                p = jnp.exp2(s_ref[r * blk:(r + 1) * blk, :] - (m_new - rows[r]))
                lsum = lsum + jnp.sum(p, axis=0, keepdims=True)
                vj = v_ref[0, pl.ds(pl.multiple_of((group_start(t) + r) * blk, blk), blk), self.cs]
                pv = pv + _dot_tn(vj, p.astype(BF16))
            return m_new, lsum, pv

        def first_group(self):
            sa_ref = self.sa_ref
            rows0 = []
            for r in range(G):
                j = group_start(0) + r
                dist = i_hi - j
                sa_ref[r * blk:(r + 1) * blk, :] = (sa_ref[r * blk:(r + 1) * blk, :]
                                                    + self.bias_ref[jnp.clip(dist, 0, 3)])
                own_lanes = jnp.where(dist == 0, nq, jnp.where(dist == 1, blk, 0))
                rows0.append(jnp.where(lane < own_lanes, 0.0, self.mask_row(j, i_hi)))
            m0, l0, pv0 = self.softmax_pv(0, sa_ref, rows0, None)
            self.m_ref[...] = m0
            self.l_ref[...] = l0
            self.acc_ref[...] = pv0

        def far_step(self, t, s_ref, next_ref):
            if next_ref is not None:
                self.logits_group(t + 1, next_ref)
            j_top = i_hi - G * t
            rows = [self.mask_row(group_start(t) + r, j_top) + self.far_bias for r in range(G)]
            m_old = self.m_ref[...]
            m_new, lsum, pv = self.softmax_pv(t, s_ref, rows, m_old)
            alpha = jnp.exp2(m_old - m_new)
            self.l_ref[...] = alpha * self.l_ref[...] + lsum
            self.acc_ref[...] = alpha * self.acc_ref[...] + pv
            self.m_ref[...] = m_new

        def finish(self):
            out_t = self.acc_ref[...] / self.l_ref[...]
            yb_ref[:, self.cs] = out_t.T.astype(yb_ref.dtype)

    heads = [Head(hh) for hh in range(hp)]

    @pl.when(step == 0)
    def _():
        for hd in heads:
            hd.prepare()

    for hd in heads:
        hd.choose_blocks()
        hd.logits_group(0, hd.sa_ref)
        hd.logits_group(1, hd.sb_ref)
    for hd in heads:
        hd.first_group()

    def pair_body(u, carry):
        for hd in heads:
            hd.far_step(2 * u + 1, hd.sb_ref, hd.sa_ref)
        for hd in heads:
            hd.far_step(2 * u + 2, hd.sa_ref, hd.sb_ref)
        return carry
    lax.fori_loop(0, n_far // 2, pair_body, 0)

    @pl.when(n_far % 2 == 1)
    def _():
        for hd in heads:
            hd.far_step(n_far, hd.sb_ref, None)

    for hd in heads:
        hd.finish()


def _moba_bias_kernel(rb_ref, bkt_own_ref, bkt_prev_ref, bias_ref):
    h = pl.program_id(0)
    blk = MOBA_BLOCK
    nq = MOBA_QBLOCKS * blk
    far_bias = rb_ref[NUM_BUCKETS - 1, h] * LOG2E
    bo = bkt_own_ref[...]
    bp = bkt_prev_ref[...]
    own = jnp.full(bo.shape, NEG, F32)
    prev = jnp.zeros(bp.shape, F32)
    for bucket in range(NUM_BUCKETS):
        val = rb_ref[bucket, h] * LOG2E
        own = jnp.where(bo == bucket, val, own)
        prev = jnp.where(bp == bucket, val, prev)
    bias_ref[0, 0, :, 0:blk] = jnp.full((blk, blk), NEG, F32)
    bias_ref[0, 0, :, blk:nq] = own
    bias_ref[0, 1, :, 0:blk] = own
    bias_ref[0, 1, :, blk:nq] = prev
    bias_ref[0, 2, :, 0:blk] = prev
    bias_ref[0, 2, :, blk:nq] = jnp.full((blk, blk), far_bias, F32)
    bias_ref[0, 3] = jnp.full((blk, nq), far_bias, F32)


def _moba_bias(rel_bias):
    blk = MOBA_BLOCK
    nq = MOBA_QBLOCKS * blk
    pos = jnp.arange(blk, dtype=jnp.int32)
    rel_own = pos[None, :] - pos[:, None]
    bkt_own = jnp.where(rel_own >= 0, _t5_bucket(rel_own), -1)
    bkt_prev = _t5_bucket(rel_own + blk)
    full2 = pl.BlockSpec((blk, blk), lambda h: (0, 0))
    return pl.pallas_call(
        _moba_bias_kernel,
        grid=(MOBA_HEADS,),
        in_specs=[pl.BlockSpec(memory_space=pltpu.SMEM), full2, full2],
        out_specs=pl.BlockSpec((1, 4, blk, nq), lambda h: (h, 0, 0, 0)),
        out_shape=jax.ShapeDtypeStruct((MOBA_HEADS, 4, blk, nq), F32),
        compiler_params=pltpu.CompilerParams(dimension_semantics=("arbitrary",)),
        name="moba_bias",
    )(rel_bias, bkt_own, bkt_prev)


def _moba(qkvb, rel_bias, bias, batch, seq_len):
    blk = MOBA_BLOCK
    dh = MOBA_HEAD_DIM
    hp = MOBA_HEADS_PER_STEP
    ng = MOBA_HEADS // hp
    nb = seq_len // blk
    nq = MOBA_QBLOCKS * blk
    n_steps = nb // MOBA_QBLOCKS
    m_rows = qkvb.shape[0]
    qkvb3 = qkvb.reshape(batch, seq_len, 3 * MOBA_WIDTH)
    member = (jnp.arange(seq_len, dtype=jnp.int32)[None, :] // blk
              == jnp.arange(nb, dtype=jnp.int32)[:, None]).astype(BF16)
    per_head_scratch = [pltpu.VMEM((nb, dh), F32),
                        pltpu.VMEM((MASK_PAD + nb, nq), F32),
                        pltpu.VMEM((MOBA_GROUP * blk, nq), F32), pltpu.VMEM((MOBA_GROUP * blk, nq), F32),
                        pltpu.VMEM((1, nq), F32), pltpu.VMEM((1, nq), F32), pltpu.VMEM((dh, nq), F32)]
    return pl.pallas_call(
        functools.partial(_moba_kernel, nb=nb),
        grid=(batch, ng, n_steps),
        in_specs=[pl.BlockSpec(memory_space=pltpu.SMEM),
                  pl.BlockSpec((nq, hp * dh), lambda b, g, i: (b * n_steps + i, g)),
                  pl.BlockSpec((1, seq_len, hp * dh), lambda b, g, i: (b, 0, ng + g)),
                  pl.BlockSpec((1, seq_len, hp * dh), lambda b, g, i: (b, 0, 2 * ng + g)),
                  pl.BlockSpec((hp, 4, blk, nq), lambda b, g, i: (g, 0, 0, 0)),
                  pl.BlockSpec((nb, seq_len), lambda b, g, i: (0, 0))],
        out_specs=pl.BlockSpec((nq, hp * dh), lambda b, g, i: (b * n_steps + i, g)),
        out_shape=jax.ShapeDtypeStruct((m_rows, MOBA_WIDTH), BF16),
        scratch_shapes=per_head_scratch * hp,
        compiler_params=pltpu.CompilerParams(dimension_semantics=("arbitrary", "arbitrary", "arbitrary"),
                                             vmem_limit_bytes=VMEM_LIMIT_BYTES),
        name="moba",
    )(rel_bias, qkvb, qkvb3, qkvb3, bias, member)


def _mixtail_kernel(x_ref, ya_ref, yb_ref, ga_ref, gb_ref, wa_ref, wb_ref, wo_ref, g_ref, b_ref, out_ref):
    pa = _dot(ya_ref[...], wa_ref[...])
    pb = _dot(yb_ref[...], wb_ref[...])
    merged = (jax.nn.sigmoid(ga_ref[...].astype(F32)) * pa
              + jax.nn.sigmoid(gb_ref[...].astype(F32)) * pb)
    mix = _dot(merged.astype(BF16), wo_ref[...])
    z = ALPHA * x_ref[...] + mix
    out_ref[...] = _layer_norm_rows(z, g_ref[...], b_ref[...])


def _mix_tail(x2d, ya, yb, gg, w_a, w_b, w_o, ln_g, ln_b, tm=512):
    m_rows = x2d.shape[0]
    row = lambda j: pl.BlockSpec((tm, D_MODEL), lambda i: (i, j))
    wa = w_a.astype(BF16)
    wb = w_b.astype(BF16)
    wo = w_o.astype(BF16)
    return pl.pallas_call(
        _mixtail_kernel,
        grid=(m_rows // tm,),
        in_specs=[row(0), row(0), row(0), row(0), row(1),
                  _resident(wa.shape), _resident(wb.shape), _resident(wo.shape),
                  _resident((1, D_MODEL)), _resident((1, D_MODEL))],
        out_specs=row(0),
        out_shape=jax.ShapeDtypeStruct((m_rows, D_MODEL), F32),
        compiler_params=pltpu.CompilerParams(dimension_semantics=("arbitrary",),
                                             vmem_limit_bytes=VMEM_LIMIT_BYTES),
        name="mix_tail",
    )(x2d, ya, yb, gg, gg, wa, wb, wo, ln_g[None, :], ln_b[None, :])


def _ffn_kernel(x_ref, wup_ref, cw_ref, wdn_ref, g_ref, b_ref, out_ref,
                xb_ref, ua_ref, ub_ref, carry_ref, acc_ref, *, tm, tiles_per_seq):
    i = pl.program_id(0)
    halo = FFN_CONV - 1
    xb_ref[...] = x_ref[...].astype(BF16)

    @pl.when(i == 0)
    def _():
        carry_ref[...] = jnp.zeros_like(carry_ref)

    at_start = jnp.full((SUBLANES, 2 * FFN_CHUNK), i % tiles_per_seq, jnp.int32) == 0

    def up_proj(c, u_ref):
        u = _dot(xb_ref[...], wup_ref[c])
        u_ref[0:SUBLANES, :] = jnp.where(at_start, 0.0, carry_ref[c])
        u_ref[SUBLANES:SUBLANES + tm, :] = u
        carry_ref[c] = u[tm - SUBLANES:tm, :]

    def act_down(c, u_ref):
        cw = cw_ref[c]
        conv = cw[halo:halo + 1, :] * u_ref[SUBLANES:SUBLANES + tm, :]
        for k in range(halo):
            off = SUBLANES - halo + k
            conv = conv + cw[k:k + 1, :] * u_ref[off:off + tm, :]
        gate = conv[:, :FFN_CHUNK]
        up = conv[:, FFN_CHUNK:]
        act = (gate * jax.nn.sigmoid(gate) * up).astype(BF16)
        if c == 0:
            acc_ref[...] = _dot(act, wdn_ref[c])
        else:
            acc_ref[...] += _dot(act, wdn_ref[c])

    up_proj(0, ua_ref)
    for p in range(N_FFN_CHUNKS // 2):
        up_proj(2 * p + 1, ub_ref)
        act_down(2 * p, ua_ref)
        up_proj(2 * p + 2, ua_ref)
        act_down(2 * p + 1, ub_ref)
    act_down(N_FFN_CHUNKS - 1, ua_ref)

    z = ALPHA * x_ref[...] + acc_ref[...]
    out_ref[...] = _layer_norm_rows(z, g_ref[...], b_ref[...])


def _ffn(x2d, w_up, conv_ffn, w_down, ln_g, ln_b, seq_len, tm=512):
    m_rows = x2d.shape[0]
    nch = N_FFN_CHUNKS
    fc = FFN_CHUNK

    def pair(t):
        lead = t.shape[:-1]
        t = t.reshape(*lead, 2, nch, fc)
        t = jnp.moveaxis(t, -2, 0)
        return t.reshape(nch, *lead, 2 * fc)

    wup = pair(w_up).astype(BF16)
    cw = pair(conv_ffn)
    wdn = w_down.reshape(nch, fc, D_MODEL).astype(BF16)
    row = pl.BlockSpec((tm, D_MODEL), lambda i: (i, 0))
    return pl.pallas_call(
        functools.partial(_ffn_kernel, tm=tm, tiles_per_seq=seq_len // tm),
        grid=(m_rows // tm,),
        in_specs=[row, _resident(wup.shape), _resident(cw.shape), _resident(wdn.shape),
                  _resident((1, D_MODEL)), _resident((1, D_MODEL))],
        out_specs=row,
        out_shape=jax.ShapeDtypeStruct((m_rows, D_MODEL), F32),
        scratch_shapes=[pltpu.VMEM((tm, D_MODEL), BF16),
                        pltpu.VMEM((tm + SUBLANES, 2 * fc), F32), pltpu.VMEM((tm + SUBLANES, 2 * fc), F32),
                        pltpu.VMEM((nch, SUBLANES, 2 * fc), F32),
                        pltpu.VMEM((tm, D_MODEL), F32)],
        compiler_params=pltpu.CompilerParams(dimension_semantics=("arbitrary",),
                                             vmem_limit_bytes=VMEM_LIMIT_BYTES),
        name="ffn",
    )(x2d, wup, cw, wdn, ln_g[None, :], ln_b[None, :])


def kernel(x, w_in, b_in, conv_qk, mlstm_norm, rel_bias, w_branch_a, w_branch_b, w_out,
           ln1_g, ln1_b, w_up, conv_ffn, w_down, ln2_g, ln2_b):
    batch, seq_len, d_model = x.shape
    assert d_model == D_MODEL
    assert seq_len % (MOBA_GROUP * MOBA_BLOCK) == 0
    assert batch % MLSTM_SEQS_PER_STEP == 0
    h = x.reshape(batch * seq_len, d_model)
    moba_bias = _moba_bias(rel_bias)
    for l in range(DEPTH):
        qk, vo, ift, qkvb, gg = _in_proj(h, w_in[l], b_in[l], conv_qk[l], seq_len)
        ya = _mlstm(qk, vo, ift, mlstm_norm[l][None, :], batch, seq_len)
        yb = _moba(qkvb, rel_bias, moba_bias, batch, seq_len)
        h = _mix_tail(h, ya, yb, gg, w_branch_a[l], w_branch_b[l], w_out[l], ln1_g[l], ln1_b[l])
        h = _ffn(h, w_up[l], conv_ffn[l], w_down[l], ln2_g[l], ln2_b[l], seq_len)
    return h.reshape(batch, seq_len, d_model)
```

```python
import functools
import math

import jax
import jax.numpy as jnp
from jax import lax
from jax.experimental import pallas as pl
from jax.experimental.pallas import tpu as pltpu

D_MODEL = 1024
DEPTH = 2
MLSTM_HEADS = 4
MLSTM_HEAD_DIM = 256
MLSTM_WIDTH = MLSTM_HEADS * MLSTM_HEAD_DIM
MLSTM_CONV = 4
MOBA_HEADS = 8
MOBA_HEAD_DIM = 128
MOBA_WIDTH = MOBA_HEADS * MOBA_HEAD_DIM
MOBA_BLOCK = 256
MOBA_TOPK = 3
NUM_BUCKETS = 32
REL_MAX_DISTANCE = 128
D_FF = 2816
FFN_CONV = 3
ALPHA = (2 * DEPTH) ** 0.25
LN_EPS = 1e-5

SUBLANES = 8
LANES = 128
VMEM_LIMIT_BYTES = 56 * 1024 * 1024

MLSTM_CHUNK = 256
MLSTM_SEQS_PER_STEP = 1
FFN_CHUNK = 256
N_FFN_CHUNKS = D_FF // FFN_CHUNK
PROJ_CHUNK = 512
NEG = -1e30
MOBA_GROUP = 4
MOBA_QBLOCKS = 2
MOBA_HEADS_PER_STEP = 2
LOG2E = math.log2(math.e)
MASK_PAD = SUBLANES

BF16 = jnp.bfloat16
F32 = jnp.float32


def _dot(a, b):
    return jnp.dot(a, b, preferred_element_type=F32)


def _dot_nt(a, b):
    return lax.dot_general(a, b, (((1,), (1,)), ((), ())), preferred_element_type=F32)


def _dot_tn(a, b):
    return lax.dot_general(a, b, (((0,), (0,)), ((), ())), preferred_element_type=F32)


def _resident(shape):
    nd = len(shape)
    return pl.BlockSpec(shape, lambda *_: (0,) * nd, pipeline_mode=pl.Buffered(1))


def _layer_norm_rows(z, g, b):
    mu = jnp.mean(z, axis=-1, keepdims=True)
    zc = z - mu
    var = jnp.mean(zc * zc, axis=-1, keepdims=True)
    return zc * lax.rsqrt(var + LN_EPS) * g + b


def _inproj_kernel(x_ref, wqk_ref, bqk_ref, cw_ref, wvo_ref, bvo_ref, wif_ref, bif_ref,
                   wb_ref, bb_ref, wg_ref, bg_ref,
                   qk_ref, vo_ref, if_ref, qkvb_ref, gg_ref, xb_ref, *ubuf_refs, tm, tiles_per_seq):
    i = pl.program_id(0)
    xb_ref[...] = x_ref[...].astype(BF16)
    halo = MLSTM_CONV - 1

    @pl.when(i == 0)
    def _():
        for ubuf_ref in ubuf_refs:
            ubuf_ref[tm:tm + SUBLANES, :] = jnp.zeros((SUBLANES, PROJ_CHUNK), F32)

    at_start = jnp.full((SUBLANES, PROJ_CHUNK), i % tiles_per_seq, jnp.int32) == 0

    def conv_chunk(c):
        ubuf_ref = ubuf_refs[c]
        cs = slice(c * PROJ_CHUNK, (c + 1) * PROJ_CHUNK)
        u = _dot(xb_ref[...],wqk_ref[:, cs]) + bqk_ref[:, cs]
        ubuf_ref[0:SUBLANES, :] = jnp.where(at_start, 0.0, ubuf_ref[tm:tm + SUBLANES, :])
        ubuf_ref[SUBLANES:SUBLANES + tm, :] = u
        conv = cw_ref[halo:halo + 1, cs] * u
        for k in range(halo):
            off = SUBLANES - halo + k
            conv = conv + cw_ref[k:k + 1, cs] * ubuf_ref[off:off + tm, :]
        act = conv * jax.nn.sigmoid(conv)
        if c * PROJ_CHUNK >= MLSTM_WIDTH:
            act = act * (MLSTM_HEAD_DIM ** -0.5)
        qk_ref[:, cs] = act.astype(qk_ref.dtype)

    def plain_chunk(w_ref, b_ref, o_ref, c):
        cs = slice(c * PROJ_CHUNK, (c + 1) * PROJ_CHUNK)
        o_ref[:, cs] = (_dot(xb_ref[...],w_ref[:, cs]) + b_ref[:, cs]).astype(o_ref.dtype)

    plain = ([(wvo_ref, bvo_ref, vo_ref, c) for c in range(2 * MLSTM_WIDTH // PROJ_CHUNK)]
             + [(wb_ref, bb_ref, qkvb_ref, c) for c in range(3 * MOBA_WIDTH // PROJ_CHUNK)]
             + [(wg_ref, bg_ref, gg_ref, c) for c in range(2 * D_MODEL // PROJ_CHUNK)])
    per_conv = -(-len(plain) // len(ubuf_refs))
    for c in range(len(ubuf_refs)):
        conv_chunk(c)
        for job in plain[c * per_conv:(c + 1) * per_conv]:
            plain_chunk(*job)
    if_ref[...] = _dot(xb_ref[...],wif_ref[...]) + bif_ref[...]


def _in_proj(x2d, w_in, b_in, conv_qk, seq_len, tm=512):
    m_rows = x2d.shape[0]
    w4 = 4 * MLSTM_WIDTH
    nh = MLSTM_HEADS
    b0 = w4 + 2 * nh
    scale_b = MOBA_HEAD_DIM ** -0.5 * LOG2E
    col_scale = jnp.concatenate([jnp.full((MOBA_WIDTH,), scale_b, F32), jnp.ones((2 * MOBA_WIDTH,), F32)])
    wqk = w_in[:, :2 * MLSTM_WIDTH].astype(BF16)
    bqk = b_in[None, :2 * MLSTM_WIDTH]
    wvo = w_in[:, 2 * MLSTM_WIDTH:w4].astype(BF16)
    bvo = b_in[None, 2 * MLSTM_WIDTH:w4]
    wif = jnp.pad(w_in[:, w4:b0], ((0, 0), (0, LANES - 2 * nh))).astype(BF16)
    bif = jnp.pad(b_in[w4:b0], (0, LANES - 2 * nh))[None, :]
    wb = (w_in[:, b0:b0 + 3 * MOBA_WIDTH] * col_scale).astype(BF16)
    bb = (b_in[b0:b0 + 3 * MOBA_WIDTH] * col_scale)[None, :]
    wg = w_in[:, b0 + 3 * MOBA_WIDTH:].astype(BF16)
    bg = b_in[None, b0 + 3 * MOBA_WIDTH:]

    row = lambda n: pl.BlockSpec((tm, n), lambda i: (i, 0))
    n_chunks = 2 * MLSTM_WIDTH // PROJ_CHUNK
    return pl.pallas_call(
        functools.partial(_inproj_kernel, tm=tm, tiles_per_seq=seq_len // tm),
        grid=(m_rows // tm,),
        in_specs=[row(D_MODEL),
                  _resident(wqk.shape), _resident(bqk.shape), _resident(conv_qk.shape),
                  _resident(wvo.shape), _resident(bvo.shape),
                  _resident(wif.shape), _resident(bif.shape),
                  _resident(wb.shape), _resident(bb.shape),
                  _resident(wg.shape), _resident(bg.shape)],
        out_specs=[row(2 * MLSTM_WIDTH), row(2 * MLSTM_WIDTH), row(LANES), row(3 * MOBA_WIDTH), row(2 * D_MODEL)],
        out_shape=[jax.ShapeDtypeStruct((m_rows, 2 * MLSTM_WIDTH), BF16),
                   jax.ShapeDtypeStruct((m_rows, 2 * MLSTM_WIDTH), BF16),
                   jax.ShapeDtypeStruct((m_rows, LANES), F32),
                   jax.ShapeDtypeStruct((m_rows, 3 * MOBA_WIDTH), BF16),
                   jax.ShapeDtypeStruct((m_rows, 2 * D_MODEL), BF16)],
        scratch_shapes=[pltpu.VMEM((tm, D_MODEL), BF16)] + [pltpu.VMEM((tm + SUBLANES, PROJ_CHUNK), F32)] * n_chunks,
        compiler_params=pltpu.CompilerParams(dimension_semantics=("arbitrary",),
                                             vmem_limit_bytes=VMEM_LIMIT_BYTES),
        name="in_proj",
    )(x2d, wqk, bqk, conv_qk, wvo, bvo, wif, bif, wb, bb, wg, bg)


def _log_sigmoid(x):
    return jnp.minimum(x, 0.0) - jnp.log1p(jnp.exp(-jnp.abs(x)))


def _mlstm_kernel(qk_refs, vo_refs, ift_ref, gain_ref, ya_refs, *state_refs):
    c = pl.program_id(1)
    L = MLSTM_CHUNK
    nh = MLSTM_HEADS
    dh = MLSTM_HEAD_DIM
    n_chains = MLSTM_SEQS_PER_STEP * nh
    c_refs, n_refs, m_refs = state_refs[:n_chains], state_refs[n_chains:2 * n_chains], state_refs[2 * n_chains:]

    @pl.when(c == 0)
    def _():
        for ref in state_refs:
            ref[...] = jnp.zeros_like(ref)

    ri = lax.broadcasted_iota(jnp.int32, (L, L), 0)
    ci = lax.broadcasted_iota(jnp.int32, (L, L), 1)
    eye = ri == ci
    causal = ri >= ci
    lane = lax.broadcasted_iota(jnp.int32, (2 * nh, L), 1)

    gates = []
    for bb in range(MLSTM_SEQS_PER_STEP):
        rows = ift_ref[bb]
        cum = _log_sigmoid(rows)
        shift = 1
        while shift < L:
            cum = cum + jnp.where(lane >= shift, pltpu.roll(cum, shift, axis=1), 0.0)
            shift *= 2
        gates.append((rows, cum))

    for chain in range(n_chains):
        bb, h = divmod(chain, nh)
        rows, cum = gates[bb]
        qk_ref, vo_ref, ya_ref = qk_refs.at[bb], vo_refs.at[bb], ya_refs.at[bb]
        c_ref, n_ref, m_ref = c_refs[chain], n_refs[chain], m_refs[chain]
        hs = slice(h * dh, (h + 1) * dh)
        li_row = rows[h:h + 1, :]
        b_row = cum[nh + h:nh + h + 1, :]
        b_col = jnp.sum(jnp.where(eye, b_row, 0.0), axis=1, keepdims=True)
        li_col = jnp.sum(jnp.where(eye, li_row, 0.0), axis=1, keepdims=True)

        m_prev = m_ref[...]
        d_intra = jnp.where(causal, b_col - b_row + li_row, -jnp.inf)
        inter = b_col + m_prev
        m_q = jnp.maximum(inter, jnp.max(d_intra, axis=1, keepdims=True))
        w_inter = jnp.exp(inter - m_q)

        qb = qk_ref[:, hs]
        kb = qk_ref[:, MLSTM_WIDTH + h * dh:MLSTM_WIDTH + (h + 1) * dh]
        vb = vo_ref[:, hs]
        s = _dot_nt(qb, kb) * jnp.exp(d_intra - m_q)
        num = w_inter * _dot(qb, c_ref[...].astype(BF16)) + _dot(s.astype(BF16), vb)
        qn = jnp.sum(qb.astype(F32) * n_ref[...], axis=1, keepdims=True)
        den = w_inter * qn + jnp.sum(s, axis=1, keepdims=True)
        hh = num / jnp.maximum(jnp.abs(den), jnp.exp(-m_q))

        mu = jnp.mean(hh, axis=1, keepdims=True)
        hc = hh - mu
        var = jnp.mean(hc * hc, axis=1, keepdims=True)
        hn = hc * lax.rsqrt(var + LN_EPS) * gain_ref[:, hs]
        og = vo_ref[:, MLSTM_WIDTH + h * dh:MLSTM_WIDTH + (h + 1) * dh].astype(F32)
        ya_ref[:, hs] = (jax.nn.sigmoid(og) * hn).astype(ya_ref.dtype)

        b_last = b_row[:, L - 1:L]
        d_state_row = b_last - b_row + li_row
        m_new = jnp.maximum(b_last + m_prev, jnp.max(d_state_row, axis=1, keepdims=True))
        w_prev = jnp.exp(b_last + m_prev - m_new)
        w_k = jnp.exp(b_last - b_col + li_col - m_new)
        kw = kb.astype(F32) * w_k
        c_ref[...] = w_prev * c_ref[...] + _dot(kw.T.astype(BF16), vb)
        n_ref[...] = w_prev * n_ref[...] + jnp.sum(kw, axis=0, keepdims=True)
        m_ref[...] = m_new


def _mlstm(qk, vo, ift, gain, batch, seq_len):
    L = MLSTM_CHUNK
    dh = MLSTM_HEAD_DIM
    nh = MLSTM_HEADS
    nc = seq_len // L
    m_rows = qk.shape[0]
    ns = MLSTM_SEQS_PER_STEP
    n_chains = ns * nh
    rows = lambda n: pl.BlockSpec((ns, L, n), lambda b, c: (b, c, 0))
    ya = pl.pallas_call(
        _mlstm_kernel,
        grid=(batch // ns, nc),
        in_specs=[rows(2 * MLSTM_WIDTH), rows(2 * MLSTM_WIDTH),
                  pl.BlockSpec((ns, 2 * nh, L), lambda b, c: (b, 0, c)),
                  pl.BlockSpec((1, MLSTM_WIDTH), lambda b, c: (0, 0))],
        out_specs=rows(MLSTM_WIDTH),
        out_shape=jax.ShapeDtypeStruct((batch, seq_len, MLSTM_WIDTH), BF16),
        scratch_shapes=([pltpu.VMEM((dh, dh), F32)] * n_chains + [pltpu.VMEM((1, dh), F32)] * n_chains
                        + [pltpu.VMEM((1, 1), F32)] * n_chains),
        compiler_params=pltpu.CompilerParams(dimension_semantics=("arbitrary", "arbitrary"),
                                             vmem_limit_bytes=VMEM_LIMIT_BYTES),
        name="mlstm",
    )(qk.reshape(batch, seq_len, 2 * MLSTM_WIDTH), vo.reshape(batch, seq_len, 2 * MLSTM_WIDTH), ift, gain)
    return ya.reshape(m_rows, MLSTM_WIDTH)


def _t5_bucket(rel):
    n = jnp.maximum(rel, 0)
    max_exact = NUM_BUCKETS // 2
    nf = jnp.maximum(n, max_exact).astype(F32)
    large = max_exact + (jnp.log(nf / max_exact) / math.log(REL_MAX_DISTANCE / max_exact)
                         * (NUM_BUCKETS - max_exact)).astype(jnp.int32)
    large = jnp.minimum(large, NUM_BUCKETS - 1)
    return jnp.where(n < max_exact, n, large)


def _moba_kernel(rb_ref, q_ref, k_ref, v_ref, bias_refs, member_ref, yb_ref, *scratch, nb):
    hp = MOBA_HEADS_PER_STEP
    per_head = len(scratch) // hp
    step = pl.program_id(2)
    blk = MOBA_BLOCK
    dh = MOBA_HEAD_DIM
    nq = MOBA_QBLOCKS * blk
    G = MOBA_GROUP
    i_lo = MOBA_QBLOCKS * step
    i_hi = i_lo + 1
    lane = lax.broadcasted_iota(jnp.int32, (1, nq), 1)
    n_far = (i_hi + G) // G - 1

    def group_start(t):
        return jnp.maximum(i_hi - G * t - (G - 1), 0)

    class Head:
        def __init__(self, hh):
            (self.kmean_ref, self.mask_ref, self.sa_ref, self.sb_ref,
             self.m_ref, self.l_ref, self.acc_ref) = scratch[hh * per_head:(hh + 1) * per_head]
            self.bias_ref = bias_refs.at[hh]
            self.cs = slice(hh * dh, (hh + 1) * dh)
            self.h = hp * pl.program_id(1) + hh
            self.far_bias = rb_ref[NUM_BUCKETS - 1, self.h] * LOG2E

        def prepare(self):
            self.kmean_ref[...] = _dot(member_ref[...], k_ref[0, :, self.cs]) * (1.0 / blk)

        def choose_blocks(self):
            qb = q_ref[:, self.cs]
            km = self.kmean_ref[...]
            km_hi = km.astype(BF16)
            r1 = km - km_hi.astype(F32)
            km_mid = r1.astype(BF16)
            km_lo = (r1 - km_mid.astype(F32)).astype(BF16)
            gate = _dot_nt(km_hi, qb) + _dot_nt(km_mid, qb) + _dot_nt(km_lo, qb)
            row = lax.broadcasted_iota(jnp.int32, gate.shape, 0)
            own_blk = jnp.where(lane < blk, i_lo, i_hi)
            gate = jnp.where(row < own_blk, gate, -jnp.inf)
            chosen = jnp.zeros(gate.shape, jnp.bool_)
            for r in range(MOBA_TOPK):
                mx = jnp.max(gate, axis=0, keepdims=True)
                first = jnp.min(jnp.where(gate == mx, row, nb), axis=0, keepdims=True)
                pick = row == first + jnp.where(own_blk > r, 0, 2 * nb)
                chosen = jnp.logical_or(chosen, pick)
                gate = jnp.where(pick, -jnp.inf, gate)
            self.mask_ref[0:MASK_PAD, :] = jnp.full((MASK_PAD, nq), NEG, F32)
            self.mask_ref[MASK_PAD:MASK_PAD + nb, :] = jnp.where(chosen, 0.0, NEG)

        def mask_row(self, j, j_top):
            return self.mask_ref[pl.ds(jnp.where(j <= j_top, j + MASK_PAD, 0), 1), :]

        def logits_group(self, t, s_ref):
            qb = q_ref[:, self.cs]
            kg = k_ref[0, pl.ds(pl.multiple_of(group_start(t) * blk, blk), G * blk), self.cs]
            s_ref[...] = _dot_nt(kg, qb)

        def softmax_pv(self, t, s_ref, rows, m_old):
            cmax = [jnp.max(s_ref[r * blk:(r + 1) * blk, :], axis=0, keepdims=True) + rows[r] for r in range(G)]
            m_new = functools.reduce(jnp.maximum, cmax)
            if m_old is not None:
                m_new = jnp.maximum(m_new, m_old)
            lsum = jnp.zeros((1, nq), F32)
            pv = jnp.zeros((dh, nq), F32)
            for r in range(G):
                p = jnp.exp2(s_ref[r * blk:(r + 1) * blk, :] - (m_new - rows[r]))
                lsum = lsum + jnp.sum(p, axis=0, keepdims=True)
                vj = v_ref[0, pl.ds(pl.multiple_of((group_start(t) + r) * blk, blk), blk), self.cs]
                pv = pv + _dot_tn(vj, p.astype(BF16))
            return m_new, lsum, pv

        def first_group(self):
            sa_ref = self.sa_ref
            rows0 = []
            for r in range(G):
                j = group_start(0) + r
                dist = i_hi - j
                sa_ref[r * blk:(r + 1) * blk, :] = (sa_ref[r * blk:(r + 1) * blk, :]
                                                    + self.bias_ref[jnp.clip(dist, 0, 3)])
                own_lanes = jnp.where(dist == 0, nq, jnp.where(dist == 1, blk, 0))
                rows0.append(jnp.where(lane < own_lanes, 0.0, self.mask_row(j, i_hi)))
            m0, l0, pv0 = self.softmax_pv(0, sa_ref, rows0, None)
            self.m_ref[...] = m0
            self.l_ref[...] = l0
            self.acc_ref[...] = pv0

        def far_step(self, t, s_ref, next_ref):
            if next_ref is not None:
                self.logits_group(t + 1, next_ref)
            j_top = i_hi - G * t
            rows = [self.mask_row(group_start(t) + r, j_top) + self.far_bias for r in range(G)]
            m_old = self.m_ref[...]
            m_new, lsum, pv = self.softmax_pv(t, s_ref, rows, m_old)
            alpha = jnp.exp2(m_old - m_new)
            self.l_ref[...] = alpha * self.l_ref[...] + lsum
            self.acc_ref[...] = alpha * self.acc_ref[...] + pv
            self.m_ref[...] = m_new

        def finish(self):
            out_t = self.acc_ref[...] / self.l_ref[...]
            yb_ref[:, self.cs] = out_t.T.astype(yb_ref.dtype)

    heads = [Head(hh) for hh in range(hp)]

    @pl.when(step == 0)
    def _():
        for hd in heads:
            hd.prepare()

    for hd in heads:
        hd.choose_blocks()
        hd.logits_group(0, hd.sa_ref)
        hd.logits_group(1, hd.sb_ref)
    for hd in heads:
        hd.first_group()

    def pair_body(u, carry):
        for hd in heads:
            hd.far_step(2 * u + 1, hd.sb_ref, hd.sa_ref)
        for hd in heads:
            hd.far_step(2 * u + 2, hd.sa_ref, hd.sb_ref)
        return carry
    lax.fori_loop(0, n_far // 2, pair_body, 0)

    @pl.when(n_far % 2 == 1)
    def _():
        for hd in heads:
            hd.far_step(n_far, hd.sb_ref, None)

    for hd in heads:
        hd.finish()


def _moba_bias_kernel(rb_ref, bkt_own_ref, bkt_prev_ref, bias_ref):
    h = pl.program_id(0)
    blk = MOBA_BLOCK
    nq = MOBA_QBLOCKS * blk
    far_bias = rb_ref[NUM_BUCKETS - 1, h] * LOG2E
    bo = bkt_own_ref[...]
    bp = bkt_prev_ref[...]
    own = jnp.full(bo.shape, NEG, F32)
    prev = jnp.zeros(bp.shape, F32)
    for bucket in range(NUM_BUCKETS):
        val = rb_ref[bucket, h] * LOG2E
        own = jnp.where(bo == bucket, val, own)
        prev = jnp.where(bp == bucket, val, prev)
    bias_ref[0, 0, :, 0:blk] = jnp.full((blk, blk), NEG, F32)
    bias_ref[0, 0, :, blk:nq] = own
    bias_ref[0, 1, :, 0:blk] = own
    bias_ref[0, 1, :, blk:nq] = prev
    bias_ref[0, 2, :, 0:blk] = prev
    bias_ref[0, 2, :, blk:nq] = jnp.full((blk, blk), far_bias, F32)
    bias_ref[0, 3] = jnp.full((blk, nq), far_bias, F32)


def _moba_bias(rel_bias):
    blk = MOBA_BLOCK
    nq = MOBA_QBLOCKS * blk
    pos = jnp.arange(blk, dtype=jnp.int32)
    rel_own = pos[None, :] - pos[:, None]
    bkt_own = jnp.where(rel_own >= 0, _t5_bucket(rel_own), -1)
    bkt_prev = _t5_bucket(rel_own + blk)
    full2 = pl.BlockSpec((blk, blk), lambda h: (0, 0))
    return pl.pallas_call(
        _moba_bias_kernel,
        grid=(MOBA_HEADS,),
        in_specs=[pl.BlockSpec(memory_space=pltpu.SMEM), full2, full2],
        out_specs=pl.BlockSpec((1, 4, blk, nq), lambda h: (h, 0, 0, 0)),
        out_shape=jax.ShapeDtypeStruct((MOBA_HEADS, 4, blk, nq), F32),
        compiler_params=pltpu.CompilerParams(dimension_semantics=("arbitrary",)),
        name="moba_bias",
    )(rel_bias, bkt_own, bkt_prev)


def _moba(qkvb, rel_bias, bias, batch, seq_len):
    blk = MOBA_BLOCK
    dh = MOBA_HEAD_DIM
    hp = MOBA_HEADS_PER_STEP
    ng = MOBA_HEADS // hp
    nb = seq_len // blk
    nq = MOBA_QBLOCKS * blk
    n_steps = nb // MOBA_QBLOCKS
    m_rows = qkvb.shape[0]
    qkvb3 = qkvb.reshape(batch, seq_len, 3 * MOBA_WIDTH)
    member = (jnp.arange(seq_len, dtype=jnp.int32)[None, :] // blk
              == jnp.arange(nb, dtype=jnp.int32)[:, None]).astype(BF16)
    per_head_scratch = [pltpu.VMEM((nb, dh), F32),
                        pltpu.VMEM((MASK_PAD + nb, nq), F32),
                        pltpu.VMEM((MOBA_GROUP * blk, nq), F32), pltpu.VMEM((MOBA_GROUP * blk, nq), F32),
                        pltpu.VMEM((1, nq), F32), pltpu.VMEM((1, nq), F32), pltpu.VMEM((dh, nq), F32)]
    return pl.pallas_call(
        functools.partial(_moba_kernel, nb=nb),
        grid=(batch, ng, n_steps),
        in_specs=[pl.BlockSpec(memory_space=pltpu.SMEM),
                  pl.BlockSpec((nq, hp * dh), lambda b, g, i: (b * n_steps + i, g)),
                  pl.BlockSpec((1, seq_len, hp * dh), lambda b, g, i: (b, 0, ng + g)),
                  pl.BlockSpec((1, seq_len, hp * dh), lambda b, g, i: (b, 0, 2 * ng + g)),
                  pl.BlockSpec((hp, 4, blk, nq), lambda b, g, i: (g, 0, 0, 0)),
                  pl.BlockSpec((nb, seq_len), lambda b, g, i: (0, 0))],
        out_specs=pl.BlockSpec((nq, hp * dh), lambda b, g, i: (b * n_steps + i, g)),
        out_shape=jax.ShapeDtypeStruct((m_rows, MOBA_WIDTH), BF16),
        scratch_shapes=per_head_scratch * hp,
        compiler_params=pltpu.CompilerParams(dimension_semantics=("arbitrary", "arbitrary", "arbitrary"),
                                             vmem_limit_bytes=VMEM_LIMIT_BYTES),
        name="moba",
    )(rel_bias, qkvb, qkvb3, qkvb3, bias, member)


def _mixtail_kernel(x_ref, ya_ref, yb_ref, ga_ref, gb_ref, wa_ref, wb_ref, wo_ref, g_ref, b_ref, out_ref):
    pa = _dot(ya_ref[...], wa_ref[...])
    pb = _dot(yb_ref[...], wb_ref[...])
    merged = (jax.nn.sigmoid(ga_ref[...].astype(F32)) * pa
              + jax.nn.sigmoid(gb_ref[...].astype(F32)) * pb)
    mix = _dot(merged.astype(BF16), wo_ref[...])
    z = ALPHA * x_ref[...] + mix
    out_ref[...] = _layer_norm_rows(z, g_ref[...], b_ref[...])


def _mix_tail(x2d, ya, yb, gg, w_a, w_b, w_o, ln_g, ln_b, tm=512):
    m_rows = x2d.shape[0]
    row = lambda j: pl.BlockSpec((tm, D_MODEL), lambda i: (i, j))
    wa = w_a.astype(BF16)
    wb = w_b.astype(BF16)
    wo = w_o.astype(BF16)
    return pl.pallas_call(
        _mixtail_kernel,
        grid=(m_rows // tm,),
        in_specs=[row(0), row(0), row(0), row(0), row(1),
                  _resident(wa.shape), _resident(wb.shape), _resident(wo.shape),
                  _resident((1, D_MODEL)), _resident((1, D_MODEL))],
        out_specs=row(0),
        out_shape=jax.ShapeDtypeStruct((m_rows, D_MODEL), F32),
        compiler_params=pltpu.CompilerParams(dimension_semantics=("arbitrary",),
                                             vmem_limit_bytes=VMEM_LIMIT_BYTES),
        name="mix_tail",
    )(x2d, ya, yb, gg, gg, wa, wb, wo, ln_g[None, :], ln_b[None, :])


def _ffn_kernel(x_ref, wup_ref, cw_ref, wdn_ref, g_ref, b_ref, out_ref,
                xb_ref, ua_ref, ub_ref, carry_ref, acc_ref, *, tm, tiles_per_seq):
    i = pl.program_id(0)
    halo = FFN_CONV - 1
    xb_ref[...] = x_ref[...].astype(BF16)

    @pl.when(i == 0)
    def _():
        carry_ref[...] = jnp.zeros_like(carry_ref)

    at_start = jnp.full((SUBLANES, 2 * FFN_CHUNK), i % tiles_per_seq, jnp.int32) == 0

    def up_proj(c, u_ref):
        u = _dot(xb_ref[...], wup_ref[c])
        u_ref[0:SUBLANES, :] = jnp.where(at_start, 0.0, carry_ref[c])
        u_ref[SUBLANES:SUBLANES + tm, :] = u
        carry_ref[c] = u[tm - SUBLANES:tm, :]

    def act_down(c, u_ref):
        cw = cw_ref[c]
        conv = cw[halo:halo + 1, :] * u_ref[SUBLANES:SUBLANES + tm, :]
        for k in range(halo):
            off = SUBLANES - halo + k
            conv = conv + cw[k:k + 1, :] * u_ref[off:off + tm, :]
        gate = conv[:, :FFN_CHUNK]
        up = conv[:, FFN_CHUNK:]
        act = (gate * jax.nn.sigmoid(gate) * up).astype(BF16)
        if c == 0:
            acc_ref[...] = _dot(act, wdn_ref[c])
        else:
            acc_ref[...] += _dot(act, wdn_ref[c])

    up_proj(0, ua_ref)
    for p in range(N_FFN_CHUNKS // 2):
        up_proj(2 * p + 1, ub_ref)
        act_down(2 * p, ua_ref)
        up_proj(2 * p + 2, ua_ref)
        act_down(2 * p + 1, ub_ref)
    act_down(N_FFN_CHUNKS - 1, ua_ref)

    z = ALPHA * x_ref[...] + acc_ref[...]
    out_ref[...] = _layer_norm_rows(z, g_ref[...], b_ref[...])


def _ffn(x2d, w_up, conv_ffn, w_down, ln_g, ln_b, seq_len, tm=512):
    m_rows = x2d.shape[0]
    nch = N_FFN_CHUNKS
    fc = FFN_CHUNK

    def pair(t):
        lead = t.shape[:-1]
        t = t.reshape(*lead, 2, nch, fc)
        t = jnp.moveaxis(t, -2, 0)
        return t.reshape(nch, *lead, 2 * fc)

    wup = pair(w_up).astype(BF16)
    cw = pair(conv_ffn)
    wdn = w_down.reshape(nch, fc, D_MODEL).astype(BF16)
    row = pl.BlockSpec((tm, D_MODEL), lambda i: (i, 0))
    return pl.pallas_call(
        functools.partial(_ffn_kernel, tm=tm, tiles_per_seq=seq_len // tm),
        grid=(m_rows // tm,),
        in_specs=[row, _resident(wup.shape), _resident(cw.shape), _resident(wdn.shape),
                  _resident((1, D_MODEL)), _resident((1, D_MODEL))],
        out_specs=row,
        out_shape=jax.ShapeDtypeStruct((m_rows, D_MODEL), F32),
        scratch_shapes=[pltpu.VMEM((tm, D_MODEL), BF16),
                        pltpu.VMEM((tm + SUBLANES, 2 * fc), F32), pltpu.VMEM((tm + SUBLANES, 2 * fc), F32),
                        pltpu.VMEM((nch, SUBLANES, 2 * fc), F32),
                        pltpu.VMEM((tm, D_MODEL), F32)],
        compiler_params=pltpu.CompilerParams(dimension_semantics=("arbitrary",),
                                             vmem_limit_bytes=VMEM_LIMIT_BYTES),
        name="ffn",
    )(x2d, wup, cw, wdn, ln_g[None, :], ln_b[None, :])


def kernel(x, w_in, b_in, conv_qk, mlstm_norm, rel_bias, w_branch_a, w_branch_b, w_out,
           ln1_g, ln1_b, w_up, conv_ffn, w_down, ln2_g, ln2_b):
    batch, seq_len, d_model = x.shape
    assert d_model == D_MODEL
    assert seq_len % (MOBA_GROUP * MOBA_BLOCK) == 0
    assert batch % MLSTM_SEQS_PER_STEP == 0
    h = x.reshape(batch * seq_len, d_model)
    moba_bias = _moba_bias(rel_bias)
    for l in range(DEPTH):
        qk, vo, ifp, qkvb, gg = _in_proj(h, w_in[l], b_in[l], conv_qk[l], seq_len)
        ift = ifp[:, :2 * MLSTM_HEADS].reshape(batch, seq_len, 2 * MLSTM_HEADS).transpose(0, 2, 1)
        ya = _mlstm(qk, vo, ift, mlstm_norm[l][None, :], batch, seq_len)
        yb = _moba(qkvb, rel_bias, moba_bias, batch, seq_len)
        h = _mix_tail(h, ya, yb, gg, w_branch_a[l], w_branch_b[l], w_out[l], ln1_g[l], ln1_b[l])
        h = _ffn(h, w_up[l], conv_ffn[l], w_down[l], ln2_g[l], ln2_b[l], seq_len)
    return h.reshape(batch, seq_len, d_model)
```
